```python
import jax, jax.numpy as jnp
from jax import lax
import numpy as np

D_MODEL = 1024
BATCH = 32
SEQ = 256
DEPTH = 4
DEC_BATCH = 4
DEC_SEQ = 2048
PAST_LEN = 256

GRID_W = 64
MLA_HEADS = 8
QK_NOPE_DIM = 64
QK_ROPE_DIM = 32
QK_HEAD_DIM = QK_NOPE_DIM + QK_ROPE_DIM
V_HEAD_DIM = 64
KV_LORA_RANK = 256
MLA_WIDTH = MLA_HEADS * V_HEAD_DIM
RET_HEADS = 4
RET_DK = 128
RET_DV = 128
RET_CHUNK = 128
RET_WIDTH = RET_HEADS * RET_DV
MIX_WIDTH = MLA_WIDTH + RET_WIDTH
IN_SIZES = (MLA_HEADS * QK_HEAD_DIM, KV_LORA_RANK, QK_ROPE_DIM,
            RET_HEADS * RET_DK, RET_HEADS * RET_DK, RET_HEADS * RET_DV, RET_WIDTH)
IN_COLS = sum(IN_SIZES)
D_FF = 2816
N_EXPERTS = 8
TOP_K = 2
EXPERT_FF = 1408
N_DENSE = (DEPTH + 1) // 2
N_MOE = DEPTH // 2
Q_BLOCK = 128
ROPE_BASE = 10000.0
EPS = 1e-6

kernel_name = "hybrid_mla_retention_diffusion_step"


def rmsnorm(x, g=None):
    xf = x.astype(jnp.float32)
    y = (xf * lax.rsqrt(jnp.mean(xf * xf, axis=-1, keepdims=True) + EPS)).astype(x.dtype)
    if g is not None:
        y = y * g
    return y


def axial_rope_tables(n_tok, dim, dtype):
    rows = n_tok // GRID_W
    row = jnp.repeat(jnp.arange(rows), GRID_W)
    col = jnp.tile(jnp.arange(GRID_W), rows)
    half = dim // 2
    freqs = ROPE_BASE ** (-jnp.arange(0, half, 2, dtype=jnp.float32) / half)

    def ang(p):
        a = p.astype(jnp.float32)[:, None] * freqs[None, :]
        return jnp.concatenate([a, a], axis=-1)

    angles = jnp.concatenate([ang(row), ang(col)], axis=-1)
    return jnp.cos(angles).astype(dtype), jnp.sin(angles).astype(dtype)


def apply_rope(x, cos, sin):
    half = x.shape[-1] // 2

    def rot_half(v):
        v1, v2 = jnp.split(v, 2, axis=-1)
        return jnp.concatenate([-v2, v1], axis=-1)

    rot = jnp.concatenate([rot_half(x[..., :half]), rot_half(x[..., half:])], axis=-1)
    return x * cos[None, :, None, :] + rot * sin[None, :, None, :]


def split_in_proj(p):
    idx = [int(v) for v in np.cumsum(IN_SIZES)[:-1]]
    return jnp.split(p, idx, axis=-1)


def adaln(cvec, w_ada, b_ada):
    mod = jax.nn.silu(cvec) @ w_ada + b_ada
    return [m[:, None, :] for m in jnp.split(mod, 6, axis=-1)]


def modulate(xn, shift, scale):
    return xn * (1.0 + scale) + shift


def block_attention(q, k, v):
    B, Tq, H, d = q.shape
    dv = v.shape[-1]
    nb = Tq // Q_BLOCK
    qb = q.reshape(B, nb, Q_BLOCK, H, d).transpose(1, 0, 2, 3, 4)
    scale = d ** -0.5

    def one(qi):
        s = jnp.einsum('bqhd,bkhd->bhqk', qi, k).astype(jnp.float32) * scale
        p = jax.nn.softmax(s, axis=-1).astype(v.dtype)
        return jnp.einsum('bhqk,bkhe->bqhe', p, v)

    out = lax.map(one, qb)
    return out.transpose(1, 0, 2, 3, 4).reshape(B, Tq, H, dv)


def mla_kv(ckv, krope, w_ukv, k_norm):
    B, L, _ = ckv.shape
    kv = (ckv @ w_ukv).reshape(B, L, MLA_HEADS, QK_NOPE_DIM + V_HEAD_DIM)
    k_nope, v = kv[..., :QK_NOPE_DIM], kv[..., QK_NOPE_DIM:]
    k_pe = jnp.broadcast_to(krope[:, :, None, :], (B, L, MLA_HEADS, QK_ROPE_DIM))
    k = rmsnorm(jnp.concatenate([k_nope, k_pe], axis=-1), k_norm)
    return k, v


def rope_tail(x, cos, sin):
    return jnp.concatenate([x[..., :QK_NOPE_DIM], apply_rope(x[..., QK_NOPE_DIM:], cos, sin)], axis=-1)


def retention_chunked(q, k, v, log_gamma, s0):
    B, T, H, dk = q.shape
    dv = v.shape[-1]
    C = RET_CHUNK
    nc = T // C

    def chunks(a):
        return a.astype(jnp.float32).reshape(B, nc, C, H, a.shape[-1]).transpose(1, 0, 3, 2, 4)

    qc, kc, vc = chunks(q), chunks(k), chunks(v)
    pos = jnp.arange(C, dtype=jnp.float32)
    diff = pos[:, None] - pos[None, :]
    lg = log_gamma[:, None, None]
    decay_mask = jnp.where(diff >= 0, jnp.exp(lg * jnp.maximum(diff, 0.0)), 0.0)
    q_decay = jnp.exp(log_gamma[:, None] * (pos[None, :] + 1.0))
    k_decay = jnp.exp(log_gamma[:, None] * (C - 1.0 - pos[None, :]))
    chunk_decay = jnp.exp(log_gamma * C)

    def step(S, inp):
        qi, ki, vi = inp
        scores = jnp.einsum('bhid,bhjd->bhij', qi, ki) * decay_mask[None]
        intra = jnp.einsum('bhij,bhje->bhie', scores, vi)
        cross = jnp.einsum('bhid,bhde->bhie', qi, S) * q_decay[None, :, :, None]
        S_new = S * chunk_decay[None, :, None, None] + jnp.einsum(
            'bhjd,bhje->bhde', ki * k_decay[None, :, :, None], vi)
        return S_new, intra + cross

    S_fin, out = lax.scan(step, s0.astype(jnp.float32), (qc, kc, vc))
    out = out.transpose(1, 0, 3, 2, 4).reshape(B, T, H, dv)
    return out, S_fin


def bidir_retention(q, k, v, decay_logit, s0_f, s0_b):
    lg = jax.nn.log_sigmoid(decay_logit.astype(jnp.float32))
    o_f, s_f = retention_chunked(q, k, v, lg[0], s0_f)
    o_b, s_b = retention_chunked(jnp.flip(q, 1), jnp.flip(k, 1), jnp.flip(v, 1), lg[1], s0_b)
    return o_f + jnp.flip(o_b, 1), s_f, s_b


def retention_output(r, g):
    B, T = r.shape[0], r.shape[1]
    r = rmsnorm(r).reshape(B, T, RET_WIDTH).astype(g.dtype)
    return r * jax.nn.silu(g)


def mix_context(h, w_in, kv_norm, w_ukv, q_norm, k_norm, decay_logit, w_out):
    B, T, _ = h.shape
    q, ckv, krope, qr, kr, vr, g = split_in_proj(h @ w_in)
    q = rmsnorm(q.reshape(B, T, MLA_HEADS, QK_HEAD_DIM), q_norm)
    ckv = rmsnorm(ckv, kv_norm)
    k, v = mla_kv(ckv, krope, w_ukv, k_norm)
    a = block_attention(q, k, v).reshape(B, T, MLA_WIDTH)
    qr = qr.reshape(B, T, RET_HEADS, RET_DK)
    kr = kr.reshape(B, T, RET_HEADS, RET_DK) * (RET_DK ** -0.5)
    vr = vr.reshape(B, T, RET_HEADS, RET_DV)
    zero = jnp.zeros((B, RET_HEADS, RET_DK, RET_DV), jnp.float32)
    r, s_f, s_b = bidir_retention(qr, kr, vr, decay_logit, zero, zero)
    out = jnp.concatenate([a, retention_output(r, g)], axis=-1) @ w_out
    state = jnp.stack([s_f, s_b], axis=1).astype(h.dtype)
    return out, ckv, krope, state


def mix_latent(h, ckv_ctx, krope_ctx, state_ctx, w_in, kv_norm, w_ukv, q_norm, k_norm, decay_logit, w_out):
    B, T, _ = h.shape
    cos_m, sin_m = axial_rope_tables(T, QK_ROPE_DIM, h.dtype)
    cos_r, sin_r = axial_rope_tables(T, RET_DK, h.dtype)
    q, ckv, krope, qr, kr, vr, g = split_in_proj(h @ w_in)
    q = rope_tail(rmsnorm(q.reshape(B, T, MLA_HEADS, QK_HEAD_DIM), q_norm), cos_m, sin_m)
    ckv = rmsnorm(ckv, kv_norm)
    k_lat, v_lat = mla_kv(ckv, krope, w_ukv, k_norm)
    k_lat = rope_tail(k_lat, cos_m, sin_m)
    k_ctx, v_ctx = mla_kv(ckv_ctx, krope_ctx, w_ukv, k_norm)
    k = jnp.concatenate([k_ctx, k_lat], axis=1)
    v = jnp.concatenate([v_ctx, v_lat], axis=1)
    a = block_attention(q, k, v).reshape(B, T, MLA_WIDTH)
    qr = apply_rope(qr.reshape(B, T, RET_HEADS, RET_DK), cos_r, sin_r)
    kr = apply_rope(kr.reshape(B, T, RET_HEADS, RET_DK), cos_r, sin_r) * (RET_DK ** -0.5)
    vr = vr.reshape(B, T, RET_HEADS, RET_DV)
    r, _, _ = bidir_retention(qr, kr, vr, decay_logit, state_ctx[:, 0], state_ctx[:, 1])
    return jnp.concatenate([a, retention_output(r, g)], axis=-1) @ w_out


def swiglu(x, wg, wu, wd):
    return (jax.nn.silu(x @ wg) * (x @ wu)) @ wd


def moe_ffn(h, w_router, wg, wu, wd):
    B, T, D = h.shape
    x = h.reshape(B * T, D)
    logits = (x @ w_router).astype(jnp.float32)
    vals, idx = lax.top_k(logits, TOP_K)
    w = jax.nn.softmax(vals, axis=-1)
    combine = jnp.sum(jax.nn.one_hot(idx, N_EXPERTS, dtype=jnp.float32) * w[..., None], axis=1).astype(h.dtype)
    out = jnp.zeros_like(x)
    for e in range(N_EXPERTS):
        out = out + combine[:, e:e + 1] * swiglu(x, wg[e], wu[e], wd[e])
    return out.reshape(B, T, D)


def channel_mixer(h, l, w_ffn_gate, w_ffn_up, w_ffn_down, w_router, w_exp_gate, w_exp_up, w_exp_down):
    if l % 2 == 0:
        i = l // 2
        return swiglu(h, w_ffn_gate[i], w_ffn_up[i], w_ffn_down[i])
    i = l // 2
    return moe_ffn(h, w_router[i], w_exp_gate[i], w_exp_up[i], w_exp_down[i])


def setup_inputs(seed: int = 0) -> dict:
    key = jax.random.key(seed)
    ks = jax.random.split(key, 32)
    f32 = jnp.float32

    def nrm(k, shape, scale=1.0):
        return jax.random.normal(k, shape, f32) * scale

    base_logit = jnp.log(2.0 ** (5.0 + jnp.arange(RET_HEADS, dtype=f32)) - 1.0)
    return {
        "x_prompt": nrm(ks[0], (BATCH, SEQ, D_MODEL)),
        "x_sample": nrm(ks[1], (DEC_BATCH, DEC_SEQ, D_MODEL)),
        "cache_ckv": nrm(ks[2], (DEC_BATCH, DEPTH, PAST_LEN, KV_LORA_RANK)),
        "cache_krope": nrm(ks[3], (DEC_BATCH, DEPTH, PAST_LEN, QK_ROPE_DIM)),
        "state_ret": nrm(ks[4], (DEC_BATCH, DEPTH, 2, RET_HEADS, RET_DK, RET_DV), 0.5),
        "c": nrm(ks[5], (DEC_BATCH, D_MODEL)),
        "c_ctx": nrm(ks[6], (D_MODEL,)),
        "attn_norm": 1.0 + nrm(ks[7], (DEPTH, D_MODEL), 0.02),
        "ffn_norm": 1.0 + nrm(ks[8], (DEPTH, D_MODEL), 0.02),
        "w_ada": nrm(ks[9], (DEPTH, D_MODEL, 6 * D_MODEL), 0.5 * D_MODEL ** -0.5),
        "b_ada": nrm(ks[10], (DEPTH, 6 * D_MODEL), 0.02),
        "w_in": nrm(ks[11], (DEPTH, D_MODEL, IN_COLS), D_MODEL ** -0.5),
        "kv_norm": 1.0 + nrm(ks[12], (DEPTH, KV_LORA_RANK), 0.02),
        "w_ukv": nrm(ks[13], (DEPTH, KV_LORA_RANK, MLA_HEADS * (QK_NOPE_DIM + V_HEAD_DIM)), KV_LORA_RANK ** -0.5),
        "q_norm": 1.0 + nrm(ks[14], (DEPTH, QK_HEAD_DIM), 0.02),
        "k_norm": 1.0 + nrm(ks[15], (DEPTH, QK_HEAD_DIM), 0.02),
        "decay_logit": base_logit[None, None, :] + nrm(ks[16], (DEPTH, 2, RET_HEADS), 0.1),
        "w_out": nrm(ks[17], (DEPTH, MIX_WIDTH, D_MODEL), MIX_WIDTH ** -0.5),
        "w_ffn_gate": nrm(ks[18], (N_DENSE, D_MODEL, D_FF), D_MODEL ** -0.5),
        "w_ffn_up": nrm(ks[19], (N_DENSE, D_MODEL, D_FF), D_MODEL ** -0.5),
        "w_ffn_down": nrm(ks[20], (N_DENSE, D_FF, D_MODEL), D_FF ** -0.5),
        "w_router": nrm(ks[21], (N_MOE, D_MODEL, N_EXPERTS), D_MODEL ** -0.5),
        "w_exp_gate": nrm(ks[22], (N_MOE, N_EXPERTS, D_MODEL, EXPERT_FF), D_MODEL ** -0.5),
        "w_exp_up": nrm(ks[23], (N_MOE, N_EXPERTS, D_MODEL, EXPERT_FF), D_MODEL ** -0.5),
        "w_exp_down": nrm(ks[24], (N_MOE, N_EXPERTS, EXPERT_FF, D_MODEL), EXPERT_FF ** -0.5),
    }


def reference(x_prompt, x_sample, cache_ckv, cache_krope, state_ret, c, c_ctx,
              attn_norm, ffn_norm, w_ada, b_ada, w_in, kv_norm, w_ukv, q_norm, k_norm,
              decay_logit, w_out, w_ffn_gate, w_ffn_up, w_ffn_down,
              w_router, w_exp_gate, w_exp_up, w_exp_down):
    yp = x_prompt
    ys = x_sample
    ckv_list, krope_list, state_list = [], [], []
    for l in range(DEPTH):
        mix_w = (w_in[l], kv_norm[l], w_ukv[l], q_norm[l], k_norm[l], decay_logit[l], w_out[l])
        ffn_w = (w_ffn_gate, w_ffn_up, w_ffn_down, w_router, w_exp_gate, w_exp_up, w_exp_down)
        sh1, sc1, g1, sh2, sc2, g2 = adaln(c_ctx[None, :], w_ada[l], b_ada[l])
        h = modulate(rmsnorm(yp, attn_norm[l]), sh1, sc1)
        out, ckv_l, krope_l, st_l = mix_context(h, *mix_w)
        yp = yp + g1 * out
        h = modulate(rmsnorm(yp, ffn_norm[l]), sh2, sc2)
        yp = yp + g2 * channel_mixer(h, l, *ffn_w)
        ckv_list.append(ckv_l)
        krope_list.append(krope_l)
        state_list.append(st_l)
        sh1, sc1, g1, sh2, sc2, g2 = adaln(c, w_ada[l], b_ada[l])
        h = modulate(rmsnorm(ys, attn_norm[l]), sh1, sc1)
        out = mix_latent(h, cache_ckv[:, l], cache_krope[:, l], state_ret[:, l], *mix_w)
        ys = ys + g1 * out
        h = modulate(rmsnorm(ys, ffn_norm[l]), sh2, sc2)
        ys = ys + g2 * channel_mixer(h, l, *ffn_w)
    new_cache_ckv = jnp.stack(ckv_list, axis=1)
    new_cache_krope = jnp.stack(krope_list, axis=1)
    new_state_ret = jnp.stack(state_list, axis=1)
    return (yp, ys, new_cache_ckv, new_cache_krope, new_state_ret)
```

```python
import functools
import math

import jax
import jax.numpy as jnp
import numpy as np
from jax import lax
from jax.experimental import pallas as pl
from jax.experimental.pallas import tpu as pltpu

D_MODEL = 1024
DEPTH = 4
GRID_W = 64
MLA_HEADS = 8
QK_NOPE_DIM = 64
QK_ROPE_DIM = 32
QK_HEAD_DIM = QK_NOPE_DIM + QK_ROPE_DIM
V_HEAD_DIM = 64
KV_LORA_RANK = 256
MLA_WIDTH = MLA_HEADS * V_HEAD_DIM
RET_HEADS = 4
RET_DK = 128
RET_DV = 128
RET_CHUNK = 128
RET_WIDTH = RET_HEADS * RET_DV
D_FF = 2816
N_EXPERTS = 8
EXPERT_FF = 1408
ROPE_BASE = 10000.0
EPS = 1e-6

LANES = 128
QP_WIDTH = MLA_HEADS * LANES
OFF_Q = 0
OFF_CKV = OFF_Q + QP_WIDTH
OFF_KR = OFF_CKV + KV_LORA_RANK
OFF_QR = OFF_KR + LANES
OFF_KRET = OFF_QR + RET_WIDTH
OFF_VR = OFF_KRET + RET_WIDTH
OFF_G = OFF_VR + RET_WIDTH
IN_COLS_P = OFF_G + RET_WIDTH

VMEM_LIMIT = 56 * 1024 * 1024
LOG2E = math.log2(math.e)

F32 = jnp.float32
BF16 = jnp.bfloat16


def _dot(a, b):
    return jnp.dot(a, b, preferred_element_type=F32)


def _dot_nt(a, b):
    return lax.dot_general(a, b, (((1,), (1,)), ((), ())), preferred_element_type=F32)


def _dot_tn(a, b):
    return lax.dot_general(a, b, (((0,), (0,)), ((), ())), preferred_element_type=F32)


def _rms(x, n):
    return x * lax.rsqrt(jnp.sum(x * x, axis=-1, keepdims=True) * (1.0 / n) + EPS)


def _silu(x):
    return x * (1.0 / (1.0 + jnp.exp(-x)))


def _rope(t, cos, sa, sb, q):
    return t * cos + pltpu.roll(t, LANES - q, 1) * sa + pltpu.roll(t, q, 1) * sb


def _params(sem):
    return pltpu.CompilerParams(dimension_semantics=sem, vmem_limit_bytes=VMEM_LIMIT)


def _adaln_body(c_ref, w_ref, b_ref, o_ref):
    x = _silu(c_ref[...]).astype(BF16)
    o_ref[0] = _dot(x, w_ref[0].astype(BF16)) + b_ref[0]


def _adaln(cs, w_ada, b_ada):
    tn = 1536
    nb = cs.shape[0]
    return pl.pallas_call(
        _adaln_body,
        grid=(DEPTH, 6 * D_MODEL // tn),
        in_specs=[
            pl.BlockSpec((nb, D_MODEL), lambda l, j: (0, 0)),
            pl.BlockSpec((1, D_MODEL, tn), lambda l, j: (l, 0, j)),
            pl.BlockSpec((1, 1, tn), lambda l, j: (l, 0, j)),
        ],
        out_specs=pl.BlockSpec((1, nb, tn), lambda l, j: (l, 0, j)),
        out_shape=jax.ShapeDtypeStruct((DEPTH, nb, 6 * D_MODEL), F32),
        compiler_params=_params(("arbitrary", "arbitrary")),
        name="adaln",
    )(cs, w_ada, b_ada.reshape(DEPTH, 1, 6 * D_MODEL))


def _inproj_body(rope, y_ref, gain_ref, sh_ref, sc_ref, w_ref, qg_ref, kvg_ref, *rest):
    if rope:
        cm_ref, sam_ref, sbm_ref, cr_ref, sar_ref, sbr_ref = rest[:6]
        rest = rest[6:]
    q_ref, ckv_ref, kr_ref, qr_ref, kret_ref, vr_ref, g_ref = rest

    x = y_ref[...]
    h = _rms(x, D_MODEL) * gain_ref[...]
    h = h * (1.0 + sc_ref[0]) + sh_ref[0]
    hb = h.astype(BF16)

    qg = qg_ref[...]
    for hd in range(MLA_HEADS):
        c0 = OFF_Q + hd * LANES
        t = _dot(hb, w_ref[:, c0:c0 + LANES])
        t = _rms(t, QK_HEAD_DIM) * qg
        if rope:
            t = _rope(t, cm_ref[...], sam_ref[...], sbm_ref[...], QK_ROPE_DIM // 4)
        q_ref[:, hd * LANES:(hd + 1) * LANES] = t.astype(BF16)

    t = _dot(hb, w_ref[:, OFF_CKV:OFF_CKV + KV_LORA_RANK])
    ckv_ref[...] = _rms(t, KV_LORA_RANK) * kvg_ref[...]

    kr_ref[...] = _dot(hb, w_ref[:, OFF_KR:OFF_KR + LANES])

    for hd in range(RET_HEADS):
        c0 = hd * LANES
        t = _dot(hb, w_ref[:, OFF_QR + c0:OFF_QR + c0 + LANES])
        if rope:
            t = _rope(t, cr_ref[...], sar_ref[...], sbr_ref[...], RET_DK // 4)
        qr_ref[:, c0:c0 + LANES] = t.astype(BF16)
        t = _dot(hb, w_ref[:, OFF_KRET + c0:OFF_KRET + c0 + LANES])
        if rope:
            t = _rope(t, cr_ref[...], sar_ref[...], sbr_ref[...], RET_DK // 4)
        kret_ref[:, c0:c0 + LANES] = (t * (RET_DK ** -0.5)).astype(BF16)
    vr_ref[...] = _dot(hb, w_ref[:, OFF_VR:OFF_VR + RET_WIDTH]).astype(BF16)
    g_ref[...] = _dot(hb, w_ref[:, OFF_G:OFF_G + RET_WIDTH]).astype(BF16)


def _inproj(y, gain, sh, sc, wp, qg, kvg, tables, rows_per_batch, tm):
    rows = y.shape[0]
    rope = tables is not None
    bpb = rows_per_batch // tm
    row = lambda i: (i, 0)
    const = lambda i: (0, 0)
    mod = lambda i: (i // bpb, 0, 0)
    in_specs = [
        pl.BlockSpec((tm, D_MODEL), row),
        pl.BlockSpec((1, D_MODEL), const),
        pl.BlockSpec((1, 1, D_MODEL), mod),
        pl.BlockSpec((1, 1, D_MODEL), mod),
        pl.BlockSpec((D_MODEL, IN_COLS_P), const),
        pl.BlockSpec((1, LANES), const),
        pl.BlockSpec((1, KV_LORA_RANK), const),
    ]
    args = [y, gain, sh, sc, wp, qg, kvg]
    if rope:
        tab = lambda i: (i % bpb, 0)
        in_specs += [pl.BlockSpec((tm, LANES), tab)] * 6
        args += list(tables)
    widths = (QP_WIDTH, KV_LORA_RANK, LANES, RET_WIDTH, RET_WIDTH, RET_WIDTH, RET_WIDTH)
    dtypes = (BF16, F32, F32, BF16, BF16, BF16, BF16)
    return pl.pallas_call(
        functools.partial(_inproj_body, rope),
        grid=(rows // tm,),
        in_specs=in_specs,
        out_specs=[pl.BlockSpec((tm, w), row) for w in widths],
        out_shape=[jax.ShapeDtypeStruct((rows, w), d) for w, d in zip(widths, dtypes)],
        compiler_params=_params(("arbitrary",)),
        name="inproj_rope" if rope else "inproj",
    )(*args)


def _make_kv(ckv_f32, kr_pad, wk_ref, wv_ref, kg, rope_tabs):
    cb = ckv_f32.astype(BF16)
    ks, vs = [], []
    for hd in range(MLA_HEADS):
        k = _dot(cb, wk_ref[:, hd * LANES:(hd + 1) * LANES]) + kr_pad
        k = _rms(k, QK_HEAD_DIM) * kg
        if rope_tabs is not None:
            k = _rope(k, *rope_tabs, QK_ROPE_DIM // 4)
        ks.append(k.astype(BF16))
        vs.append(_dot(cb, wv_ref[:, hd * LANES:(hd + 1) * LANES]).astype(BF16))
    return ks, vs


def _softmax_pv(q, k, v):
    s = _dot_nt(q, k)
    m = jnp.max(s, axis=-1, keepdims=True)
    p = jnp.exp2((s - m) * (QK_HEAD_DIM ** -0.5 * LOG2E))
    l = jnp.sum(p, axis=-1, keepdims=True)
    return _dot(p.astype(BF16), v) * (1.0 / l)


def _attn_ctx_body(q_ref, ckv_ref, kr_ref, wk_ref, wv_ref, kg_ref, o_ref):
    ks, vs = _make_kv(ckv_ref[...], kr_ref[...], wk_ref, wv_ref, kg_ref[...], None)
    for pair in range(MLA_HEADS // 2):
        acc = None
        for hd in (2 * pair, 2 * pair + 1):
            o = _softmax_pv(q_ref[:, hd * LANES:(hd + 1) * LANES], ks[hd], vs[hd])
            acc = o if acc is None else acc + o
        o_ref[:, pair * LANES:(pair + 1) * LANES] = acc.astype(BF16)


def _attn_ctx(q, ckv, kr, wk, wv, kg, seq):
    rows = q.shape[0]
    row = lambda b: (b, 0)
    const = lambda b: (0, 0)
    return pl.pallas_call(
        _attn_ctx_body,
        grid=(rows // seq,),
        in_specs=[
            pl.BlockSpec((seq, QP_WIDTH), row),
            pl.BlockSpec((seq, KV_LORA_RANK), row),
            pl.BlockSpec((seq, LANES), row),
            pl.BlockSpec((KV_LORA_RANK, QP_WIDTH), const),
            pl.BlockSpec((KV_LORA_RANK, QP_WIDTH), const),
            pl.BlockSpec((1, LANES), const),
        ],
        out_specs=pl.BlockSpec((seq, MLA_WIDTH), row),
        out_shape=jax.ShapeDtypeStruct((rows, MLA_WIDTH), BF16),
        compiler_params=_params(("arbitrary",)),
        name="attn_ctx",
    )(q, ckv, kr, wk, wv, kg)


def _attn_lat_body(past, seq, tq, q_ref, ckv_ref, kr_ref, cckv_ref, ckr_ref, wk_ref, wv_ref, kg_ref,
                   cm_ref, sam_ref, sbm_ref, o_ref, k_s, v_s):
    kc = 256

    @pl.when(pl.program_id(1) == 0)
    def _():
        kg = kg_ref[...]
        for c in range(past // kc):
            r = slice(c * kc, (c + 1) * kc)
            ks, vs = _make_kv(cckv_ref[0, r, :], ckr_ref[0, r, :], wk_ref, wv_ref, kg, None)
            for hd in range(MLA_HEADS):
                k_s[hd, r, :] = ks[hd]
                v_s[hd, r, :] = vs[hd]

        def chunk(c, carry):
            r0 = pl.multiple_of(c * kc, kc)
            rs = pl.ds(r0, kc)
            tabs = (cm_ref[rs, :], sam_ref[rs, :], sbm_ref[rs, :])
            ks, vs = _make_kv(ckv_ref[rs, :], kr_ref[rs, :], wk_ref, wv_ref, kg, tabs)
            ro = pl.ds(past + r0, kc)
            for hd in range(MLA_HEADS):
                k_s[hd, ro, :] = ks[hd]
                v_s[hd, ro, :] = vs[hd]
            return carry

        lax.fori_loop(0, seq // kc, chunk, 0)

    for pair in range(MLA_HEADS // 2):
        acc = None
        for hd in (2 * pair, 2 * pair + 1):
            o = _softmax_pv(q_ref[:, hd * LANES:(hd + 1) * LANES], k_s[hd], v_s[hd])
            acc = o if acc is None else acc + o
        o_ref[:, pair * LANES:(pair + 1) * LANES] = acc.astype(BF16)


def _attn_lat(q, ckv, kr, cache_ckv_l, cache_kr_l, wk, wv, kg, tabs_m, seq, tq):
    rows = q.shape[0]
    nb = rows // seq
    nq = seq // tq
    past = cache_ckv_l.shape[1]
    const = lambda b, i: (0, 0)
    return pl.pallas_call(
        functools.partial(_attn_lat_body, past, seq, tq),
        grid=(nb, nq),
        in_specs=[
            pl.BlockSpec((tq, QP_WIDTH), lambda b, i: (b * nq + i, 0)),
            pl.BlockSpec((seq, KV_LORA_RANK), lambda b, i: (b, 0)),
            pl.BlockSpec((seq, LANES), lambda b, i: (b, 0)),
            pl.BlockSpec((1, past, KV_LORA_RANK), lambda b, i: (b, 0, 0)),
            pl.BlockSpec((1, past, LANES), lambda b, i: (b, 0, 0)),
            pl.BlockSpec((KV_LORA_RANK, QP_WIDTH), const),
            pl.BlockSpec((KV_LORA_RANK, QP_WIDTH), const),
            pl.BlockSpec((1, LANES), const),
            pl.BlockSpec((seq, LANES), const),
            pl.BlockSpec((seq, LANES), const),
            pl.BlockSpec((seq, LANES), const),
        ],
        out_specs=pl.BlockSpec((tq, MLA_WIDTH), lambda b, i: (b * nq + i, 0)),
        out_shape=jax.ShapeDtypeStruct((rows, MLA_WIDTH), BF16),
        scratch_shapes=[
            pltpu.VMEM((MLA_HEADS, past + seq, LANES), BF16),
            pltpu.VMEM((MLA_HEADS, past + seq, LANES), BF16),
        ],
        compiler_params=_params(("arbitrary", "arbitrary")),
        name="attn_lat",
    )(q, ckv, kr, cache_ckv_l, cache_kr_l, wk, wv, kg, *tabs_m)


def _ret_body(seq, has_state, lg_ref, q_ref, k_ref, v_ref, g_ref, *rest):
    if has_state:
        s0_ref, o_ref, acc_ref, sf_ref, sb_ref = rest
    else:
        o_ref, st_ref, acc_ref, sf_ref, sb_ref = rest
    C = RET_CHUNK
    nc = seq // C
    ii = lax.broadcasted_iota(jnp.int32, (C, C), 0)
    jj = lax.broadcasted_iota(jnp.int32, (C, C), 1)
    diff = (ii - jj).astype(F32)
    pos = lax.broadcasted_iota(jnp.int32, (C, 1), 0).astype(F32)
    cfull = jnp.full((1, LANES), float(C), F32)

    for hd in range(RET_HEADS):
        lgf = lg_ref[0, hd]
        lgb = lg_ref[1, hd]
        cols = slice(hd * LANES, (hd + 1) * LANES)
        dmask = (jnp.where(diff >= 0, jnp.exp(lgf * jnp.maximum(diff, 0.0)), 0.0)
                 + jnp.where(diff <= 0, jnp.exp(lgb * jnp.maximum(-diff, 0.0)), 0.0))
        qdf = jnp.exp(lgf * (pos + 1.0))
        kdf = jnp.exp(lgf * (C - 1.0 - pos))
        cdf = jnp.exp(lgf * cfull)
        qdb = jnp.exp(lgb * (C - pos))
        kdb = jnp.exp(lgb * pos)
        cdb = jnp.exp(lgb * cfull)

        if has_state:
            sf_ref[...] = s0_ref[0, 0, hd]
            sb_ref[...] = s0_ref[0, 1, hd]
        else:
            sf_ref[...] = jnp.zeros((RET_DK, RET_DV), F32)
            sb_ref[...] = jnp.zeros((RET_DK, RET_DV), F32)

        def fwd(c, carry):
            rs = pl.ds(pl.multiple_of(c * C, C), C)
            qc = q_ref[rs, cols]
            kc = k_ref[rs, cols]
            vc = v_ref[rs, cols]
            sc = _dot_nt(qc, kc) * dmask
            s = sf_ref[...]
            acc_ref[rs, :] = _dot(sc.astype(BF16), vc) + _dot(qc, s.astype(BF16)) * qdf
            kd = (kc.astype(F32) * kdf).astype(BF16)
            sf_ref[...] = s * cdf + _dot_tn(kd, vc)
            return carry

        lax.fori_loop(0, nc, fwd, 0)

        def bwd(t, carry):
            rs = pl.ds(pl.multiple_of((nc - 1 - t) * C, C), C)
            qc = q_ref[rs, cols]
            kc = k_ref[rs, cols]
            vc = v_ref[rs, cols]
            s = sb_ref[...]
            acc_ref[rs, :] += _dot(qc, s.astype(BF16)) * qdb
            kd = (kc.astype(F32) * kdb).astype(BF16)
            sb_ref[...] = s * cdb + _dot_tn(kd, vc)
            return carry

        lax.fori_loop(0, nc, bwd, 0)

        gh = g_ref[:, cols].astype(F32)
        o_ref[:, cols] = (_rms(acc_ref[...], RET_DV) * _silu(gh)).astype(BF16)
        if not has_state:
            st_ref[0, 0, hd] = sf_ref[...]
            st_ref[0, 1, hd] = sb_ref[...]


def _retention(lg, qr, kr, vr, g, s0, seq):
    rows = qr.shape[0]
    nb = rows // seq
    has_state = s0 is not None
    row = lambda b: (b, 0)
    st_spec = pl.BlockSpec((1, 2, RET_HEADS, RET_DK, RET_DV), lambda b: (b, 0, 0, 0, 0))
    in_specs = [pl.BlockSpec(memory_space=pltpu.SMEM)] + [pl.BlockSpec((seq, RET_WIDTH), row)] * 4
    args = [lg, qr, kr, vr, g]
    out_specs = [pl.BlockSpec((seq, RET_WIDTH), row)]
    out_shape = [jax.ShapeDtypeStruct((rows, RET_WIDTH), BF16)]
    if has_state:
        in_specs.append(st_spec)
        args.append(s0)
    else:
        out_specs.append(st_spec)
        out_shape.append(jax.ShapeDtypeStruct((nb, 2, RET_HEADS, RET_DK, RET_DV), F32))
    return pl.pallas_call(
        functools.partial(_ret_body, seq, has_state),
        grid=(nb,),
        in_specs=in_specs,
        out_specs=out_specs,
        out_shape=out_shape,
        scratch_shapes=[
            pltpu.VMEM((seq, RET_DV), F32),
            pltpu.VMEM((RET_DK, RET_DV), F32),
            pltpu.VMEM((RET_DK, RET_DV), F32),
        ],
        compiler_params=_params(("arbitrary",)),
        name="retention_lat" if has_state else "retention_ctx",
    )(*args)


def _outproj_body(y_ref, a_ref, r_ref, w_ref, g_ref, o_ref):
    out = _dot(a_ref[...], w_ref[:MLA_WIDTH, :]) + _dot(r_ref[...], w_ref[MLA_WIDTH:, :])
    o_ref[...] = y_ref[...] + g_ref[0] * out


def _outproj(y, a, r, wo, g1, rows_per_batch, tm):
    rows = y.shape[0]
    bpb = rows_per_batch // tm
    row = lambda i: (i, 0)
    return pl.pallas_call(
        _outproj_body,
        grid=(rows // tm,),
        in_specs=[
            pl.BlockSpec((tm, D_MODEL), row),
            pl.BlockSpec((tm, MLA_WIDTH), row),
            pl.BlockSpec((tm, RET_WIDTH), row),
            pl.BlockSpec((MLA_WIDTH + RET_WIDTH, D_MODEL), lambda i: (0, 0)),
            pl.BlockSpec((1, 1, D_MODEL), lambda i: (i // bpb, 0, 0)),
        ],
        out_specs=pl.BlockSpec((tm, D_MODEL), row),
        out_shape=jax.ShapeDtypeStruct((rows, D_MODEL), F32),
        compiler_params=_params(("arbitrary",)),
        name="outproj",
    )(y, a, r, wo, g1)


def _ffn_in(y_ref, gain_ref, sh_ref, sc_ref):
    h = _rms(y_ref[...], D_MODEL) * gain_ref[...]
    return h * (1.0 + sc_ref[0]) + sh_ref[0]


def _ffn_body(y_ref, gain_ref, sh_ref, sc_ref, g2_ref, wg_ref, wu_ref, wd_ref, o_ref):
    hb = _ffn_in(y_ref, gain_ref, sh_ref, sc_ref).astype(BF16)
    a = (_silu(_dot(hb, wg_ref[...])) * _dot(hb, wu_ref[...])).astype(BF16)
    o_ref[...] = y_ref[...] + g2_ref[0] * _dot(a, wd_ref[...])


def _ffn(y, gain, sh, sc, g2, wg, wu, wd, rows_per_batch, tm):
    rows = y.shape[0]
    bpb = rows_per_batch // tm
    row = lambda i: (i, 0)
    const = lambda i: (0, 0)
    mod = lambda i: (i // bpb, 0, 0)
    once = pl.Buffered(1)
    return pl.pallas_call(
        _ffn_body,
        grid=(rows // tm,),
        in_specs=[
            pl.BlockSpec((tm, D_MODEL), row),
            pl.BlockSpec((1, D_MODEL), const),
            pl.BlockSpec((1, 1, D_MODEL), mod),
            pl.BlockSpec((1, 1, D_MODEL), mod),
            pl.BlockSpec((1, 1, D_MODEL), mod),
            pl.BlockSpec((D_MODEL, D_FF), const, pipeline_mode=once),
            pl.BlockSpec((D_MODEL, D_FF), const, pipeline_mode=once),
            pl.BlockSpec((D_FF, D_MODEL), const, pipeline_mode=once),
        ],
        out_specs=pl.BlockSpec((tm, D_MODEL), row),
        out_shape=jax.ShapeDtypeStruct((rows, D_MODEL), F32),
        compiler_params=_params(("arbitrary",)),
        name="ffn",
    )(y, gain, sh, sc, g2, wg, wu, wd)


def _split_bf16(x):
    hi = x.astype(BF16)
    return hi, (x - hi.astype(F32)).astype(BF16)


def _moe_body(y_ref, gain_ref, sh_ref, sc_ref, g2_ref, wrh_ref, wrl_ref, wg_ref, wu_ref, wd_ref,
              o_ref, hb_s, comb_s, acc_s):
    e = pl.program_id(1)

    @pl.when(e == 0)
    def _():
        h = _ffn_in(y_ref, gain_ref, sh_ref, sc_ref)
        hi, lo = _split_bf16(h)
        hb_s[...] = hi
        logits = _dot(hi, wrh_ref[...]) + (_dot(lo, wrh_ref[...]) + _dot(hi, wrl_ref[...]))
        lane = lax.broadcasted_iota(jnp.int32, logits.shape, 1)
        ninf = jnp.float32(-jnp.inf)
        lg = jnp.where(lane < N_EXPERTS, logits, ninf)
        m1 = jnp.max(lg, axis=-1, keepdims=True)
        i1 = jnp.min(jnp.where(lg == m1, lane, LANES), axis=-1, keepdims=True)
        lg2 = jnp.where(lane == i1, ninf, lg)
        m2 = jnp.max(lg2, axis=-1, keepdims=True)
        i2 = jnp.min(jnp.where(lg2 == m2, lane, LANES), axis=-1, keepdims=True)
        ex = jnp.exp(m2 - m1)
        den = 1.0 / (1.0 + ex)
        comb_s[...] = jnp.where(lane == i1, den, 0.0) + jnp.where(lane == i2, ex * den, 0.0)
        acc_s[...] = jnp.zeros_like(acc_s)

    hb = hb_s[...]
    a = (_silu(_dot(hb, wg_ref[0])) * _dot(hb, wu_ref[0])).astype(BF16)
    lane = lax.broadcasted_iota(jnp.int32, comb_s.shape, 1)
    ce = jnp.sum(jnp.where(lane == e, comb_s[...], 0.0), axis=-1, keepdims=True)
    acc_s[...] += ce * _dot(a, wd_ref[0])

    @pl.when(e == N_EXPERTS - 1)
    def _():
        o_ref[...] = y_ref[...] + g2_ref[0] * acc_s[...]


def _moe(y, gain, sh, sc, g2, wrh, wrl, wg, wu, wd, rows_per_batch, tm):
    rows = y.shape[0]
    bpb = rows_per_batch // tm
    row = lambda i, e: (i, 0)
    const = lambda i, e: (0, 0)
    mod = lambda i, e: (i // bpb, 0, 0)
    return pl.pallas_call(
        _moe_body,
        grid=(rows // tm, N_EXPERTS),
        in_specs=[
            pl.BlockSpec((tm, D_MODEL), row),
            pl.BlockSpec((1, D_MODEL), const),
            pl.BlockSpec((1, 1, D_MODEL), mod),
            pl.BlockSpec((1, 1, D_MODEL), mod),
            pl.BlockSpec((1, 1, D_MODEL), mod),
            pl.BlockSpec((D_MODEL, LANES), const),
            pl.BlockSpec((D_MODEL, LANES), const),
            pl.BlockSpec((1, D_MODEL, EXPERT_FF), lambda i, e: (e, 0, 0)),
            pl.BlockSpec((1, D_MODEL, EXPERT_FF), lambda i, e: (e, 0, 0)),
            pl.BlockSpec((1, EXPERT_FF, D_MODEL), lambda i, e: (e, 0, 0)),
        ],
        out_specs=pl.BlockSpec((tm, D_MODEL), row),
        out_shape=jax.ShapeDtypeStruct((rows, D_MODEL), F32),
        scratch_shapes=[
            pltpu.VMEM((tm, D_MODEL), BF16),
            pltpu.VMEM((tm, LANES), F32),
            pltpu.VMEM((tm, D_MODEL), F32),
        ],
        compiler_params=_params(("arbitrary", "arbitrary")),
        name="moe",
    )(y, gain, sh, sc, g2, wrh, wrl, wg, wu, wd)


def _rope_tables(n_tok, dim):
    rows = n_tok // GRID_W
    row = jnp.repeat(jnp.arange(rows), GRID_W)
    col = jnp.tile(jnp.arange(GRID_W), rows)
    half = dim // 2
    freqs = ROPE_BASE ** (-jnp.arange(0, half, 2, dtype=F32) / half)

    def ang(p):
        a = p.astype(F32)[:, None] * freqs[None, :]
        return jnp.concatenate([a, a], axis=-1)

    angles = jnp.concatenate([ang(row), ang(col)], axis=-1)
    cos, sin = jnp.cos(angles), jnp.sin(angles)
    first = (np.arange(dim) % half) < (half // 2)
    sa = jnp.where(first, -sin, 0.0)
    sb = jnp.where(first, 0.0, sin)
    pad = LANES - dim
    cos = jnp.pad(cos, ((0, 0), (0, pad)), constant_values=1.0)
    sa = jnp.pad(sa, ((0, 0), (0, pad)))
    sb = jnp.pad(sb, ((0, 0), (0, pad)))
    return cos, sa, sb


def _pack_w_in(w_in):
    zeros = jnp.zeros(w_in.shape[:-1] + (LANES - QK_HEAD_DIM,), w_in.dtype)
    parts = []
    for hd in range(MLA_HEADS):
        b = hd * QK_HEAD_DIM
        parts += [w_in[..., b + QK_NOPE_DIM:b + QK_HEAD_DIM], w_in[..., b:b + QK_NOPE_DIM], zeros]
    o = MLA_HEADS * QK_HEAD_DIM
    parts.append(w_in[..., o:o + KV_LORA_RANK])
    o += KV_LORA_RANK
    parts += [w_in[..., o:o + QK_ROPE_DIM],
              jnp.zeros(w_in.shape[:-1] + (LANES - QK_ROPE_DIM,), w_in.dtype)]
    o += QK_ROPE_DIM
    parts.append(w_in[..., o:])
    return jnp.concatenate(parts, axis=-1).astype(BF16)


def _pack_w_ukv(w_ukv):
    lead = w_ukv.shape[:-1]
    z32 = jnp.zeros(lead + (QK_ROPE_DIM,), w_ukv.dtype)
    z64 = jnp.zeros(lead + (V_HEAD_DIM,), w_ukv.dtype)
    kp, vp = [], []
    for hd in range(MLA_HEADS):
        b = hd * (QK_NOPE_DIM + V_HEAD_DIM)
        kp += [z32, w_ukv[..., b:b + QK_NOPE_DIM], z32]
        v = w_ukv[..., b + QK_NOPE_DIM:b + QK_NOPE_DIM + V_HEAD_DIM]
        vp += [v, z64] if hd % 2 == 0 else [z64, v]
    return jnp.concatenate(kp, axis=-1).astype(BF16), jnp.concatenate(vp, axis=-1).astype(BF16)


def _pack_head_gain(g):
    z = jnp.zeros(g.shape[:-1] + (LANES - QK_HEAD_DIM,), g.dtype)
    return jnp.concatenate([g[..., QK_NOPE_DIM:], g[..., :QK_NOPE_DIM], z], axis=-1)


def kernel(x_prompt, x_sample, cache_ckv, cache_krope, state_ret, c, c_ctx, attn_norm, ffn_norm, w_ada, b_ada, w_in, kv_norm, w_ukv, q_norm, k_norm, decay_logit, w_out, w_ffn_gate, w_ffn_up, w_ffn_down, w_router, w_exp_gate, w_exp_up, w_exp_down):
    nb_c, seq_c, _ = x_prompt.shape
    nb_l, seq_l, _ = x_sample.shape
    tm = 512

    wp = _pack_w_in(w_in)
    wk, wv = _pack_w_ukv(w_ukv)
    wo = w_out.astype(BF16)
    wfg, wfu, wfd = w_ffn_gate.astype(BF16), w_ffn_up.astype(BF16), w_ffn_down.astype(BF16)
    weg, weu, wed = w_exp_gate.astype(BF16), w_exp_up.astype(BF16), w_exp_down.astype(BF16)
    wr = jnp.pad(w_router, ((0, 0), (0, 0), (0, LANES - N_EXPERTS)))
    wrh = wr.astype(BF16)
    wrl = (wr - wrh.astype(F32)).astype(BF16)
    qg = _pack_head_gain(q_norm)[:, None, :]
    kg = _pack_head_gain(k_norm)[:, None, :]
    kvg = kv_norm[:, None, :]
    lg = jax.nn.log_sigmoid(decay_logit.astype(F32))
    tabs_m = _rope_tables(seq_l, QK_ROPE_DIM)
    tabs_r = _rope_tables(seq_l, RET_DK)
    cache_kr_pad = jnp.pad(cache_krope, ((0, 0), (0, 0), (0, 0), (0, LANES - QK_ROPE_DIM)))

    cs = jnp.concatenate([c_ctx[None, :], c, jnp.zeros((8 - 1 - nb_l, D_MODEL), F32)], axis=0)
    mods = _adaln(cs, w_ada, b_ada).reshape(DEPTH, 8, 6, 1, D_MODEL)

    yp = x_prompt.reshape(nb_c * seq_c, D_MODEL)
    ys = x_sample.reshape(nb_l * seq_l, D_MODEL)
    ckv_list, krope_list, state_list = [], [], []
    for l in range(DEPTH):
        an, fn = attn_norm[l][None, :], ffn_norm[l][None, :]
        for ctx in (True, False):
            if ctx:
                y, m, rpb = yp, mods[l, 0:1], nb_c * seq_c
            else:
                y, m, rpb = ys, mods[l, 1:1 + nb_l], seq_l
            sh1, sc1, g1, sh2, sc2, g2 = (m[:, j] for j in range(6))
            q, ckv, kr, qr, kret, vr, g = _inproj(
                y, an, sh1, sc1, wp[l], qg[l], kvg[l], None if ctx else tabs_m + tabs_r, rpb, tm)
            if ctx:
                a = _attn_ctx(q, ckv, kr, wk[l], wv[l], kg[l], seq_c)
                r, st = _retention(lg[l], qr, kret, vr, g, None, seq_c)
                ckv_list.append(ckv.reshape(nb_c, seq_c, KV_LORA_RANK))
                krope_list.append(kr[:, :QK_ROPE_DIM].reshape(nb_c, seq_c, QK_ROPE_DIM))
                state_list.append(st)
            else:
                a = _attn_lat(q, ckv, kr, cache_ckv[:, l], cache_kr_pad[:, l], wk[l], wv[l], kg[l],
                              tabs_m, seq_l, 256)
                r, = _retention(lg[l], qr, kret, vr, g, state_ret[:, l], seq_l)
            y = _outproj(y, a, r, wo[l], g1, rpb, tm)
            i = l // 2
            if l % 2 == 0:
                y = _ffn(y, fn, sh2, sc2, g2, wfg[i], wfu[i], wfd[i], rpb, tm)
            else:
                y = _moe(y, fn, sh2, sc2, g2, wrh[i], wrl[i], weg[i], weu[i], wed[i], rpb, tm)
            if ctx:
                yp = y
            else:
                ys = y
    return (yp.reshape(nb_c, seq_c, D_MODEL), ys.reshape(nb_l, seq_l, D_MODEL),
            jnp.stack(ckv_list, axis=1), jnp.stack(krope_list, axis=1), jnp.stack(state_list, axis=1))
```

```python
import functools
import math

import jax
import jax.numpy as jnp
import numpy as np
from jax import lax
from jax.experimental import pallas as pl
from jax.experimental.pallas import tpu as pltpu

D_MODEL = 1024
DEPTH = 4
GRID_W = 64
MLA_HEADS = 8
QK_NOPE_DIM = 64
QK_ROPE_DIM = 32
QK_HEAD_DIM = QK_NOPE_DIM + QK_ROPE_DIM
V_HEAD_DIM = 64
KV_LORA_RANK = 256
MLA_WIDTH = MLA_HEADS * V_HEAD_DIM
RET_HEADS = 4
RET_DK = 128
RET_DV = 128
RET_CHUNK = 128
RET_WIDTH = RET_HEADS * RET_DV
D_FF = 2816
N_EXPERTS = 8
EXPERT_FF = 1408
ROPE_BASE = 10000.0
EPS = 1e-6

LANES = 128
SUB = 8
QP_WIDTH = MLA_HEADS * LANES
OFF_Q = 0
OFF_CKV = OFF_Q + QP_WIDTH
OFF_KR = OFF_CKV + KV_LORA_RANK
OFF_QR = OFF_KR + LANES
OFF_VR = OFF_QR + RET_WIDTH
OFF_G = OFF_VR + RET_WIDTH
IN_COLS_P = OFF_G + RET_WIDTH

VMEM_LIMIT = 56 * 1024 * 1024
LOG2E = math.log2(math.e)

ROW_BLOCK = 512
Q_BLOCK = 256
EXPERT_TILE = 256
RET_CTX_SEQS = 4

F32 = jnp.float32
BF16 = jnp.bfloat16


def _dot(a, b):
    return jnp.dot(a, b, preferred_element_type=F32)


def _dot_nt(a, b):
    return lax.dot_general(a, b, (((1,), (1,)), ((), ())), preferred_element_type=F32)


def _dot_tn(a, b):
    return lax.dot_general(a, b, (((0,), (0,)), ((), ())), preferred_element_type=F32)


def _rms(x, n):
    return x * lax.rsqrt(jnp.sum(x * x, axis=-1, keepdims=True) * (1.0 / n) + EPS)


def _silu(x):
    return x * (1.0 / (1.0 + jnp.exp(-x)))


def _rope(t, cos, sa, sb, q):
    return t * cos + pltpu.roll(t, LANES - q, 1) * sa + pltpu.roll(t, q, 1) * sb


def _params(sem, **kw):
    return pltpu.CompilerParams(dimension_semantics=sem, vmem_limit_bytes=VMEM_LIMIT, **kw)


def _adaln_body(c_ref, w_ref, b_ref, o_ref):
    x = _silu(c_ref[...]).astype(BF16)
    o_ref[0] = _dot(x, w_ref[0].astype(BF16)) + b_ref[0]


def _adaln(cs, w_ada, b_ada):
    tn = 1536
    nb = cs.shape[0]
    return pl.pallas_call(
        _adaln_body,
        grid=(DEPTH, 6 * D_MODEL // tn),
        in_specs=[
            pl.BlockSpec((nb, D_MODEL), lambda l, j: (0, 0)),
            pl.BlockSpec((1, D_MODEL, tn), lambda l, j: (l, 0, j)),
            pl.BlockSpec((1, 1, tn), lambda l, j: (l, 0, j)),
        ],
        out_specs=pl.BlockSpec((1, nb, tn), lambda l, j: (l, 0, j)),
        out_shape=jax.ShapeDtypeStruct((DEPTH, nb, 6 * D_MODEL), F32),
        compiler_params=_params(("arbitrary", "arbitrary")),
        name="adaln",
    )(cs, w_ada, b_ada.reshape(DEPTH, 1, 6 * D_MODEL))


def _inproj_body(rope, y_ref, gain_ref, sh_ref, sc_ref, w_ref, wkt_ref, qg_ref, kvg_ref, *rest):
    if rope:
        cm_ref, sam_ref, sbm_ref, cr_ref, sar_ref, sbr_ref, crt_ref, sart_ref, sbrt_ref = rest[:9]
        rest = rest[9:]
    q_ref, ckv_ref, kr_ref, qr_ref, kt_ref, vr_ref, g_ref = rest
    tm = y_ref.shape[0]

    x = y_ref[...]
    h = _rms(x, D_MODEL) * gain_ref[...]
    h = h * (1.0 + sc_ref[0]) + sh_ref[0]
    hb = h.astype(BF16)

    pq = _dot(hb, w_ref[:, OFF_Q:OFF_Q + QP_WIDTH])
    pckv = _dot(hb, w_ref[:, OFF_CKV:OFF_CKV + KV_LORA_RANK])
    kr_ref[...] = _dot(hb, w_ref[:, OFF_KR:OFF_KR + LANES])
    pqr = _dot(hb, w_ref[:, OFF_QR:OFF_QR + RET_WIDTH])
    kt = _dot_nt(wkt_ref[...], hb)
    vr_ref[...] = _dot(hb, w_ref[:, OFF_VR:OFF_VR + RET_WIDTH]).astype(BF16)
    g_ref[...] = _dot(hb, w_ref[:, OFF_G:OFF_G + RET_WIDTH]).astype(BF16)

    qg = qg_ref[...]
    for hd in range(MLA_HEADS):
        t = _rms(pq[:, hd * LANES:(hd + 1) * LANES], QK_HEAD_DIM) * qg
        if rope:
            t = _rope(t, cm_ref[...], sam_ref[...], sbm_ref[...], QK_ROPE_DIM // 4)
        q_ref[:, hd * LANES:(hd + 1) * LANES] = t.astype(BF16)

    ckv_ref[...] = _rms(pckv, KV_LORA_RANK) * kvg_ref[...]

    for hd in range(RET_HEADS):
        c0 = hd * LANES
        t = pqr[:, c0:c0 + LANES]
        if rope:
            t = _rope(t, cr_ref[...], sar_ref[...], sbr_ref[...], RET_DK // 4)
        qr_ref[:, c0:c0 + LANES] = t.astype(BF16)
    qtr = RET_DK // 4
    for hd in range(RET_HEADS):
        t = kt[hd * RET_DK:(hd + 1) * RET_DK, :]
        if rope:
            up = jnp.concatenate([t[qtr:], t[:qtr]], axis=0)
            down = jnp.concatenate([t[RET_DK - qtr:], t[:RET_DK - qtr]], axis=0)
            t = t * crt_ref[...] + up * sart_ref[...] + down * sbrt_ref[...]
        t = (t * (RET_DK ** -0.5)).astype(BF16)
        for j in range(tm // RET_CHUNK):
            kt_ref[j, hd * RET_DK:(hd + 1) * RET_DK, :] = t[:, j * RET_CHUNK:(j + 1) * RET_CHUNK]


def _inproj(y, gain, sh, sc, wp, wkt, qg, kvg, tables, rows_per_batch, tm):
    rows = y.shape[0]
    rope = tables is not None
    bpb = rows_per_batch // tm
    cpb = tm // RET_CHUNK
    row = lambda i: (i, 0)
    const = lambda i: (0, 0)
    mod = lambda i: (i // bpb, 0, 0)
    in_specs = [
        pl.BlockSpec((tm, D_MODEL), row),
        pl.BlockSpec((1, D_MODEL), const),
        pl.BlockSpec((1, 1, D_MODEL), mod),
        pl.BlockSpec((1, 1, D_MODEL), mod),
        pl.BlockSpec((D_MODEL, IN_COLS_P), const),
        pl.BlockSpec((RET_WIDTH, D_MODEL), const),
        pl.BlockSpec((1, LANES), const),
        pl.BlockSpec((1, KV_LORA_RANK), const),
    ]
    args = [y, gain, sh, sc, wp, wkt, qg, kvg]
    if rope:
        in_specs += [pl.BlockSpec((tm, LANES), lambda i: (i % bpb, 0))] * 6
        in_specs += [pl.BlockSpec((RET_DK, tm), lambda i: (0, i % bpb))] * 3
        args += list(tables)
    blocks = [(tm, QP_WIDTH), (tm, KV_LORA_RANK), (tm, LANES), (tm, RET_WIDTH),
              (cpb, RET_WIDTH, RET_CHUNK), (tm, RET_WIDTH), (tm, RET_WIDTH)]
    dtypes = (BF16, F32, F32, BF16, BF16, BF16, BF16)
    out_specs, out_shape = [], []
    for blk, dt in zip(blocks, dtypes):
        nd = len(blk)
        out_specs.append(pl.BlockSpec(blk, lambda i, nd=nd: (i,) + (0,) * (nd - 1)))
        out_shape.append(jax.ShapeDtypeStruct((rows // tm * blk[0],) + blk[1:], dt))
    return pl.pallas_call(
        functools.partial(_inproj_body, rope),
        grid=(rows // tm,),
        in_specs=in_specs,
        out_specs=out_specs,
        out_shape=out_shape,
        compiler_params=_params(("arbitrary",)),
        name="inproj_rope" if rope else "inproj",
    )(*args)


def _make_kv(ckv_f32, kr_pad, wk_ref, wv_ref, kg, rope_tabs):
    cb = ckv_f32.astype(BF16)
    kall = _dot(cb, wk_ref[...])
    vall = _dot(cb, wv_ref[...])
    ks, vs = [], []
    for hd in range(MLA_HEADS):
        k = kall[:, hd * LANES:(hd + 1) * LANES] + kr_pad
        k = _rms(k, QK_HEAD_DIM) * kg
        if rope_tabs is not None:
            k = _rope(k, *rope_tabs, QK_ROPE_DIM // 4)
        ks.append(k.astype(BF16))
        vs.append(vall[:, hd * LANES:(hd + 1) * LANES].astype(BF16))
    return ks, vs


def _softmax_pv(s, v):
    m = jnp.max(s, axis=-1, keepdims=True)
    p = jnp.exp2((s - m) * (QK_HEAD_DIM ** -0.5 * LOG2E))
    l = jnp.sum(p, axis=-1, keepdims=True)
    return _dot(p.astype(BF16), v) * (1.0 / l)


def _attend(q_ref, get_k, get_v, o_ref):
    def scores(hd):
        return _dot_nt(q_ref[:, hd * LANES:(hd + 1) * LANES], get_k(hd))

    s_next = scores(0)
    acc = None
    for hd in range(MLA_HEADS):
        s = s_next
        if hd + 1 < MLA_HEADS:
            s_next = scores(hd + 1)
        o = _softmax_pv(s, get_v(hd))
        acc = o if hd % 2 == 0 else acc + o
        if hd % 2 == 1:
            o_ref[:, (hd // 2) * LANES:(hd // 2 + 1) * LANES] = acc.astype(BF16)


def _attn_ctx_body(q_ref, ckv_ref, kr_ref, wk_ref, wv_ref, kg_ref, o_ref):
    ks, vs = _make_kv(ckv_ref[...], kr_ref[...], wk_ref, wv_ref, kg_ref[...], None)
    _attend(q_ref, lambda hd: ks[hd], lambda hd: vs[hd], o_ref)


def _attn_ctx(q, ckv, kr, wk, wv, kg, seq):
    rows = q.shape[0]
    row = lambda b: (b, 0)
    const = lambda b: (0, 0)
    return pl.pallas_call(
        _attn_ctx_body,
        grid=(rows // seq,),
        in_specs=[
            pl.BlockSpec((seq, QP_WIDTH), row),
            pl.BlockSpec((seq, KV_LORA_RANK), row),
            pl.BlockSpec((seq, LANES), row),
            pl.BlockSpec((KV_LORA_RANK, QP_WIDTH), const),
            pl.BlockSpec((KV_LORA_RANK, QP_WIDTH), const),
            pl.BlockSpec((1, LANES), const),
        ],
        out_specs=pl.BlockSpec((seq, MLA_WIDTH), row),
        out_shape=jax.ShapeDtypeStruct((rows, MLA_WIDTH), BF16),
        compiler_params=_params(("arbitrary",)),
        name="attn_ctx",
    )(q, ckv, kr, wk, wv, kg)


def _attn_lat_body(past, seq, tq, q_ref, ckv_ref, kr_ref, cckv_ref, ckr_ref, wk_ref, wv_ref, kg_ref,
                   cm_ref, sam_ref, sbm_ref, o_ref, k_s, v_s):
    kc = 256

    @pl.when(pl.program_id(1) == 0)
    def _():
        kg = kg_ref[...]
        for c in range(past // kc):
            r = slice(c * kc, (c + 1) * kc)
            ks, vs = _make_kv(cckv_ref[0, r, :], ckr_ref[0, r, :], wk_ref, wv_ref, kg, None)
            for hd in range(MLA_HEADS):
                k_s[hd, r, :] = ks[hd]
                v_s[hd, r, :] = vs[hd]

        def chunk(c, carry):
            r0 = pl.multiple_of(c * kc, kc)
            rs = pl.ds(r0, kc)
            tabs = (cm_ref[rs, :], sam_ref[rs, :], sbm_ref[rs, :])
            ks, vs = _make_kv(ckv_ref[rs, :], kr_ref[rs, :], wk_ref, wv_ref, kg, tabs)
            ro = pl.ds(past + r0, kc)
            for hd in range(MLA_HEADS):
                k_s[hd, ro, :] = ks[hd]
                v_s[hd, ro, :] = vs[hd]
            return carry

        lax.fori_loop(0, seq // kc, chunk, 0)

    _attend(q_ref, lambda hd: k_s[hd], lambda hd: v_s[hd], o_ref)


def _attn_lat(q, ckv, kr, cache_ckv_l, cache_kr_l, wk, wv, kg, tabs_m, seq, tq):
    rows = q.shape[0]
    nb = rows // seq
    nq = seq // tq
    past = cache_ckv_l.shape[1]
    const = lambda b, i: (0, 0)
    return pl.pallas_call(
        functools.partial(_attn_lat_body, past, seq, tq),
        grid=(nb, nq),
        in_specs=[
            pl.BlockSpec((tq, QP_WIDTH), lambda b, i: (b * nq + i, 0)),
            pl.BlockSpec((seq, KV_LORA_RANK), lambda b, i: (b, 0)),
            pl.BlockSpec((seq, LANES), lambda b, i: (b, 0)),
            pl.BlockSpec((1, past, KV_LORA_RANK), lambda b, i: (b, 0, 0)),
            pl.BlockSpec((1, past, LANES), lambda b, i: (b, 0, 0)),
            pl.BlockSpec((KV_LORA_RANK, QP_WIDTH), const),
            pl.BlockSpec((KV_LORA_RANK, QP_WIDTH), const),
            pl.BlockSpec((1, LANES), const),
            pl.BlockSpec((seq, LANES), const),
            pl.BlockSpec((seq, LANES), const),
            pl.BlockSpec((seq, LANES), const),
        ],
        out_specs=pl.BlockSpec((tq, MLA_WIDTH), lambda b, i: (b * nq + i, 0)),
        out_shape=jax.ShapeDtypeStruct((rows, MLA_WIDTH), BF16),
        scratch_shapes=[
            pltpu.VMEM((MLA_HEADS, past + seq, LANES), BF16),
            pltpu.VMEM((MLA_HEADS, past + seq, LANES), BF16),
        ],
        compiler_params=_params(("arbitrary", "arbitrary")),
        name="attn_lat",
    )(q, ckv, kr, cache_ckv_l, cache_kr_l, wk, wv, kg, *tabs_m)


def _ret_body(seq, nseq, has_state, lg_ref, q_ref, kt_ref, v_ref, g_ref, *rest):
    if has_state:
        s0_ref, o_ref, acc_s, u_s, st_s, dec_s = rest
    else:
        o_ref, stout_ref, acc_s, u_s, st_s, dec_s = rest
    C = RET_CHUNK
    nc = seq // C
    ii = lax.broadcasted_iota(jnp.int32, (C, C), 0)
    jj = lax.broadcasted_iota(jnp.int32, (C, C), 1)
    diff = (ii - jj).astype(F32)
    pos = ii.astype(F32)
    post = jj.astype(F32)
    cfull = jnp.full((1, LANES), float(C), F32)
    cdf, cdb = [], []
    for hd in range(RET_HEADS):
        lgf = lg_ref[0, hd]
        lgb = lg_ref[1, hd]
        dec_s[hd, 0] = (jnp.where(diff >= 0, jnp.exp(lgf * jnp.maximum(diff, 0.0)), 0.0)
                        + jnp.where(diff <= 0, jnp.exp(lgb * jnp.maximum(-diff, 0.0)), 0.0))
        dec_s[hd, 1] = jnp.exp(lgf * (pos + 1.0))
        dec_s[hd, 2] = jnp.exp(lgb * (C - pos))
        dec_s[hd, 3] = jnp.exp(lgf * (C - 1.0 - post))
        dec_s[hd, 4] = jnp.exp(lgb * post)
        cdf.append(jnp.exp(lgf * cfull))
        cdb.append(jnp.exp(lgb * cfull))

    def p1(c, carry):
        rs = pl.ds(pl.multiple_of(c * C, C), C)
        heads = range(RET_HEADS)
        cols = [slice(hd * LANES, (hd + 1) * LANES) for hd in heads]
        kts = [kt_ref[c, hd * RET_DK:(hd + 1) * RET_DK, :] for hd in heads]
        vcs = [v_ref[rs, cols[hd]] for hd in heads]
        scores = [_dot(q_ref[rs, cols[hd]], kts[hd]) for hd in heads]
        for hd in heads:
            kf = kts[hd].astype(F32)
            kd = jnp.concatenate([(kf * dec_s[hd, 3]).astype(BF16), (kf * dec_s[hd, 4]).astype(BF16)], axis=0)
            u_s[c, hd] = _dot(kd, vcs[hd])
        for hd in heads:
            acc_s[rs, cols[hd]] = _dot((scores[hd] * dec_s[hd, 0]).astype(BF16), vcs[hd])
        return carry

    lax.fori_loop(0, nseq * nc, p1, 0)

    for sq in range(nseq):
        for hd in range(RET_HEADS):
            if has_state:
                init = (s0_ref[sq, 0, hd], s0_ref[sq, 1, hd])
            else:
                init = (jnp.zeros((RET_DK, RET_DV), F32), jnp.zeros((RET_DK, RET_DV), F32))

            def p2(t, carry, sq=sq, hd=hd):
                sf, sb = carry
                cf = sq * nc + t
                cb = sq * nc + (nc - 1 - t)
                st_s[cf, hd, :, 0:RET_DV] = sf.astype(BF16)
                st_s[cb, hd, :, RET_DV:2 * RET_DV] = sb.astype(BF16)
                sf = sf * cdf[hd] + u_s[cf, hd, 0:RET_DK, :]
                sb = sb * cdb[hd] + u_s[cb, hd, RET_DK:2 * RET_DK, :]
                return sf, sb

            sf, sb = lax.fori_loop(0, nc, p2, init)
            if not has_state:
                stout_ref[sq, 0, hd] = sf
                stout_ref[sq, 1, hd] = sb

    def p3(c, carry):
        rs = pl.ds(pl.multiple_of(c * C, C), C)
        crs = [_dot(q_ref[rs, hd * LANES:(hd + 1) * LANES], st_s[c, hd]) for hd in range(RET_HEADS)]
        for hd in range(RET_HEADS):
            cols = slice(hd * LANES, (hd + 1) * LANES)
            cr = crs[hd]
            r = acc_s[rs, cols] + cr[:, 0:RET_DV] * dec_s[hd, 1] + cr[:, RET_DV:2 * RET_DV] * dec_s[hd, 2]
            gh = g_ref[rs, cols].astype(F32)
            o_ref[rs, cols] = (_rms(r, RET_DV) * _silu(gh)).astype(BF16)
        return carry

    lax.fori_loop(0, nseq * nc, p3, 0)


def _retention(lg, qr, kt, vr, g, s0, seq, nseq):
    rows = qr.shape[0]
    nb = rows // seq
    has_state = s0 is not None
    nct = nseq * seq // RET_CHUNK
    row = lambda b: (b, 0)
    st_spec = pl.BlockSpec((nseq, 2, RET_HEADS, RET_DK, RET_DV), lambda b: (b, 0, 0, 0, 0))
    tok = pl.BlockSpec((nseq * seq, RET_WIDTH), row)
    in_specs = [pl.BlockSpec(memory_space=pltpu.SMEM), tok,
                pl.BlockSpec((nct, RET_WIDTH, RET_CHUNK), lambda b: (b, 0, 0)), tok, tok]
    args = [lg, qr, kt, vr, g]
    out_specs = [pl.BlockSpec((nseq * seq, RET_WIDTH), row)]
    out_shape = [jax.ShapeDtypeStruct((rows, RET_WIDTH), BF16)]
    if has_state:
        in_specs.append(st_spec)
        args.append(s0)
    else:
        out_specs.append(st_spec)
        out_shape.append(jax.ShapeDtypeStruct((nb, 2, RET_HEADS, RET_DK, RET_DV), F32))
    return pl.pallas_call(
        functools.partial(_ret_body, seq, nseq, has_state),
        grid=(nb // nseq,),
        in_specs=in_specs,
        out_specs=out_specs,
        out_shape=out_shape,
        scratch_shapes=[
            pltpu.VMEM((nseq * seq, RET_WIDTH), F32),
            pltpu.VMEM((nct, RET_HEADS, 2 * RET_DK, RET_DV), F32),
            pltpu.VMEM((nct, RET_HEADS, RET_DK, 2 * RET_DV), BF16),
            pltpu.VMEM((RET_HEADS, 5, RET_CHUNK, RET_CHUNK), F32),
        ],
        compiler_params=_params(("arbitrary",)),
        name="retention_lat" if has_state else "retention_ctx",
    )(*args)


def _outproj_body(y_ref, a_ref, r_ref, w_ref, g_ref, o_ref):
    out = _dot(a_ref[...], w_ref[:MLA_WIDTH, :]) + _dot(r_ref[...], w_ref[MLA_WIDTH:, :])
    o_ref[...] = y_ref[...] + g_ref[0] * out


def _outproj(y, a, r, wo, g1, rows_per_batch, tm):
    rows = y.shape[0]
    bpb = rows_per_batch // tm
    row = lambda i: (i, 0)
    return pl.pallas_call(
        _outproj_body,
        grid=(rows // tm,),
        in_specs=[
            pl.BlockSpec((tm, D_MODEL), row),
            pl.BlockSpec((tm, MLA_WIDTH), row),
            pl.BlockSpec((tm, RET_WIDTH), row),
            pl.BlockSpec((MLA_WIDTH + RET_WIDTH, D_MODEL), lambda i: (0, 0)),
            pl.BlockSpec((1, 1, D_MODEL), lambda i: (i // bpb, 0, 0)),
        ],
        out_specs=pl.BlockSpec((tm, D_MODEL), row),
        out_shape=jax.ShapeDtypeStruct((rows, D_MODEL), F32),
        compiler_params=_params(("arbitrary",)),
        name="outproj",
    )(y, a, r, wo, g1)


def _ffn_in(y_ref, gain_ref, sh_ref, sc_ref):
    h = _rms(y_ref[...], D_MODEL) * gain_ref[...]
    return h * (1.0 + sc_ref[0]) + sh_ref[0]


def _ffn_body(y_ref, gain_ref, sh_ref, sc_ref, g2_ref, wg_ref, wu_ref, wd_ref, o_ref):
    hb = _ffn_in(y_ref, gain_ref, sh_ref, sc_ref).astype(BF16)
    a = (_silu(_dot(hb, wg_ref[...])) * _dot(hb, wu_ref[...])).astype(BF16)
    o_ref[...] = y_ref[...] + g2_ref[0] * _dot(a, wd_ref[...])


def _ffn(y, gain, sh, sc, g2, wg, wu, wd, rows_per_batch, tm):
    rows = y.shape[0]
    bpb = rows_per_batch // tm
    row = lambda i: (i, 0)
    const = lambda i: (0, 0)
    mod = lambda i: (i // bpb, 0, 0)
    once = pl.Buffered(1)
    return pl.pallas_call(
        _ffn_body,
        grid=(rows // tm,),
        in_specs=[
            pl.BlockSpec((tm, D_MODEL), row),
            pl.BlockSpec((1, D_MODEL), const),
            pl.BlockSpec((1, 1, D_MODEL), mod),
            pl.BlockSpec((1, 1, D_MODEL), mod),
            pl.BlockSpec((1, 1, D_MODEL), mod),
            pl.BlockSpec((D_MODEL, D_FF), const, pipeline_mode=once),
            pl.BlockSpec((D_MODEL, D_FF), const, pipeline_mode=once),
            pl.BlockSpec((D_FF, D_MODEL), const, pipeline_mode=once),
        ],
        out_specs=pl.BlockSpec((tm, D_MODEL), row),
        out_shape=jax.ShapeDtypeStruct((rows, D_MODEL), F32),
        compiler_params=_params(("arbitrary",)),
        name="ffn",
    )(y, gain, sh, sc, g2, wg, wu, wd)


def _split_bf16(x):
    hi = x.astype(BF16)
    return hi, (x - hi.astype(F32)).astype(BF16)


def _route_body(y_ref, gain_ref, sh_ref, sc_ref, wrh_ref, wrl_ref, slot_ref, w_ref, cnt_ref, carry):
    tm = y_ref.shape[0]

    @pl.when(pl.program_id(0) == 0)
    def _():
        carry[...] = jnp.zeros_like(carry)

    h = _ffn_in(y_ref, gain_ref, sh_ref, sc_ref)
    hi, lo = _split_bf16(h)
    logits = _dot(hi, wrh_ref[...]) + (_dot(lo, wrh_ref[...]) + _dot(hi, wrl_ref[...]))
    lane = lax.broadcasted_iota(jnp.int32, logits.shape, 1)
    ninf = jnp.float32(-jnp.inf)
    lg = jnp.where(lane < N_EXPERTS, logits, ninf)
    m1 = jnp.max(lg, axis=-1, keepdims=True)
    i1 = jnp.min(jnp.where(lg == m1, lane, LANES), axis=-1, keepdims=True)
    lg2 = jnp.where(lane == i1, ninf, lg)
    m2 = jnp.max(lg2, axis=-1, keepdims=True)
    i2 = jnp.min(jnp.where(lg2 == m2, lane, LANES), axis=-1, keepdims=True)
    ex = jnp.exp(m2 - m1)
    den = 1.0 / (1.0 + ex)
    w_ref[...] = jnp.where(lane == 0, den, 0.0) + jnp.where(lane == 1, ex * den, 0.0)

    sel1 = lane == i1
    sel2 = lane == i2
    onehot = jnp.where(sel1 | sel2, 1.0, 0.0)
    ri = lax.broadcasted_iota(jnp.int32, (tm, tm), 0)
    ci = lax.broadcasted_iota(jnp.int32, (tm, tm), 1)
    lower = jnp.where(ri > ci, 1.0, 0.0).astype(BF16)
    before = _dot(lower, onehot.astype(BF16)) + carry[...]
    r1 = jnp.sum(jnp.where(sel1, before, 0.0), axis=-1, keepdims=True)
    r2 = jnp.sum(jnp.where(sel2, before, 0.0), axis=-1, keepdims=True)
    slot_ref[...] = (jnp.where(lane == 0, i1, 0) + jnp.where(lane == 1, i2, 0)
                     + jnp.where(lane == 2, r1.astype(jnp.int32), 0)
                     + jnp.where(lane == 3, r2.astype(jnp.int32), 0))
    carry[...] += jnp.sum(onehot, axis=0, keepdims=True)
    cnt_ref[...] = carry[...].astype(jnp.int32)


def _route(y, gain, sh, sc, wrh, wrl, rows_per_batch, tm):
    rows = y.shape[0]
    bpb = rows_per_batch // tm
    row = lambda i: (i, 0)
    const = lambda i: (0, 0)
    mod = lambda i: (i // bpb, 0, 0)
    return pl.pallas_call(
        _route_body,
        grid=(rows // tm,),
        in_specs=[
            pl.BlockSpec((tm, D_MODEL), row),
            pl.BlockSpec((1, D_MODEL), const),
            pl.BlockSpec((1, 1, D_MODEL), mod),
            pl.BlockSpec((1, 1, D_MODEL), mod),
            pl.BlockSpec((D_MODEL, LANES), const),
            pl.BlockSpec((D_MODEL, LANES), const),
        ],
        out_specs=[pl.BlockSpec((tm, LANES), row), pl.BlockSpec((tm, LANES), row),
                   pl.BlockSpec((1, LANES), const)],
        out_shape=[jax.ShapeDtypeStruct((rows, LANES), jnp.int32), jax.ShapeDtypeStruct((rows, LANES), F32),
                   jax.ShapeDtypeStruct((1, LANES), jnp.int32)],
        scratch_shapes=[pltpu.VMEM((1, LANES), F32)],
        compiler_params=_params(("arbitrary",)),
        name="route",
    )(y, gain, sh, sc, wrh, wrl)


def _dispatch_body(tr, dest_ref, zt_ref, y_ref, gain_ref, sh_ref, sc_ref, xs_hbm, stage, zeros, sem, zsem):
    i = pl.program_id(0)
    tm = y_ref.shape[0]

    @pl.when(i == 0)
    def _():
        zeros[...] = jnp.zeros_like(zeros)

        def ztile(z):
            r0 = pl.multiple_of(zt_ref[z] * (tr * SUB), tr * SUB)
            return pltpu.make_async_copy(zeros, xs_hbm.at[pl.ds(r0, tr * SUB)], zsem)

        for z in range(zt_ref.shape[0]):
            @pl.when(zt_ref[z] >= 0)
            def _():
                ztile(z).start()
        for z in range(zt_ref.shape[0]):
            @pl.when(zt_ref[z] >= 0)
            def _():
                ztile(z).wait()

    h = _ffn_in(y_ref, gain_ref, sh_ref, sc_ref)
    for s in range(D_MODEL // LANES):
        stage[pl.ds(s, tm, stride=SUB), :] = h[:, s * LANES:(s + 1) * LANES]

    def body(t, c):
        src = stage.at[pl.ds(pl.multiple_of(t * SUB, SUB), SUB)]
        for k in range(2):
            dst = dest_ref[(i * tm + t) * 2 + k]
            pltpu.make_async_copy(src, xs_hbm.at[pl.ds(pl.multiple_of(dst * SUB, SUB), SUB)], sem).start()
        return c

    lax.fori_loop(0, tm, body, 0, unroll=8)

    for k in range(2):
        pltpu.make_async_copy(stage, stage, sem).wait()


def _dispatch(dest_flat, ztiles, y, gain, sh, sc, rows_per_batch, tm, ntile, tr):
    rows = y.shape[0]
    bpb = rows_per_batch // tm
    grid_spec = pltpu.PrefetchScalarGridSpec(
        num_scalar_prefetch=2,
        grid=(rows // tm,),
        in_specs=[
            pl.BlockSpec((tm, D_MODEL), lambda i, dr, cr: (i, 0)),
            pl.BlockSpec((1, D_MODEL), lambda i, dr, cr: (0, 0)),
            pl.BlockSpec((1, 1, D_MODEL), lambda i, dr, cr: (i // bpb, 0, 0)),
            pl.BlockSpec((1, 1, D_MODEL), lambda i, dr, cr: (i // bpb, 0, 0)),
        ],
        out_specs=pl.BlockSpec(memory_space=pl.ANY),
        scratch_shapes=[
            pltpu.VMEM((tm * SUB, LANES), F32),
            pltpu.VMEM((tr * SUB, LANES), F32),
            pltpu.SemaphoreType.DMA(()),
            pltpu.SemaphoreType.DMA(()),
        ],
    )
    return pl.pallas_call(
        functools.partial(_dispatch_body, tr),
        grid_spec=grid_spec,
        out_shape=jax.ShapeDtypeStruct((ntile * tr * SUB, LANES), F32),
        compiler_params=_params(("arbitrary",), disable_bounds_checks=True),
        name="dispatch",
    )(dest_flat, ztiles, y, gain, sh, sc)


def _experts_body(tr, te_ref, nu_ref, xs_ref, wg_ref, wu_ref, wd_ref, o_ref):
    nt = D_MODEL // LANES
    used = pl.program_id(0) < nu_ref[0]

    @pl.when(used)
    def _():
        x = jnp.concatenate([xs_ref[pl.ds(s, tr, stride=SUB), :] for s in range(nt)], axis=1).astype(BF16)
        a = (_silu(_dot(x, wg_ref[0])) * _dot(x, wu_ref[0])).astype(BF16)
        o = _dot(a, wd_ref[0])
        for s in range(nt):
            o_ref[pl.ds(s, tr, stride=SUB), :] = o[:, s * LANES:(s + 1) * LANES]

    @pl.when(jnp.logical_not(used))
    def _():
        o_ref[...] = jnp.zeros_like(o_ref)


def _experts(tile_e, nused, xs, wg, wu, wd, tr):
    ntile = tile_e.shape[0]
    blk = lambda j, te, nu: (j, 0)
    wsel = lambda j, te, nu: (te[j], 0, 0)
    grid_spec = pltpu.PrefetchScalarGridSpec(
        num_scalar_prefetch=2,
        grid=(ntile,),
        in_specs=[
            pl.BlockSpec((tr * SUB, LANES), blk),
            pl.BlockSpec((1, D_MODEL, EXPERT_FF), wsel),
            pl.BlockSpec((1, D_MODEL, EXPERT_FF), wsel),
            pl.BlockSpec((1, EXPERT_FF, D_MODEL), wsel),
        ],
        out_specs=pl.BlockSpec((tr * SUB, LANES), blk),
    )
    return pl.pallas_call(
        functools.partial(_experts_body, tr),
        grid_spec=grid_spec,
        out_shape=jax.ShapeDtypeStruct(xs.shape, F32),
        compiler_params=_params(("arbitrary",)),
        name="experts",
    )(tile_e, nused, xs, wg, wu, wd)


def _tile_plan(cnt, slots, ntile, tr):
    ptiles = (cnt + tr - 1) // tr
    tile_end = jnp.cumsum(ptiles)
    nused = tile_end[-1]
    off = (tile_end - ptiles) * tr
    experts = jnp.arange(N_EXPERTS, dtype=jnp.int32)
    dest = slots[:, 2:4] + jnp.sum(jnp.where(slots[:, 0:2, None] == experts, off, 0), axis=-1)
    j = jnp.minimum(jnp.arange(ntile), nused - 1)
    te = jnp.sum((j[:, None] >= tile_end[None, :]).astype(jnp.int32), axis=1)
    last = jnp.where(ptiles > 0, tile_end - 1, -1)
    tail = nused + jnp.arange(ntile - slots.shape[0] * 2 // tr)
    ztiles = jnp.concatenate([last, jnp.where(tail < ntile, tail, -1)])
    i32 = lambda a: a.astype(jnp.int32)
    return i32(dest).reshape(-1), i32(te), i32(nused).reshape(1), i32(ztiles)


def _combine_body(dest_ref, y_ref, g2_ref, w_ref, y2_hbm, o_ref, stage, sem):
    i = pl.program_id(0)
    tm = y_ref.shape[0]

    def body(t, c):
        for k in range(2):
            src = dest_ref[(i * tm + t) * 2 + k]
            pltpu.make_async_copy(y2_hbm.at[pl.ds(pl.multiple_of(src * SUB, SUB), SUB)],
                                  stage.at[k, pl.ds(pl.multiple_of(t * SUB, SUB), SUB)], sem).start()
        return c

    lax.fori_loop(0, tm, body, 0, unroll=8)
    w = w_ref[...]
    w1 = w[:, 0:1]
    w2 = w[:, 1:2]
    for k in range(2):
        pltpu.make_async_copy(stage.at[k], stage.at[k], sem).wait()
    parts = []
    for s in range(D_MODEL // LANES):
        parts.append(w1 * stage[0, pl.ds(s, tm, stride=SUB), :] + w2 * stage[1, pl.ds(s, tm, stride=SUB), :])
    o_ref[...] = y_ref[...] + g2_ref[0] * jnp.concatenate(parts, axis=1)


def _combine(dest_flat, y, g2, w, y2, rows_per_batch, tm):
    rows = y.shape[0]
    bpb = rows_per_batch // tm
    grid_spec = pltpu.PrefetchScalarGridSpec(
        num_scalar_prefetch=1,
        grid=(rows // tm,),
        in_specs=[
            pl.BlockSpec((tm, D_MODEL), lambda i, dr: (i, 0)),
            pl.BlockSpec((1, 1, D_MODEL), lambda i, dr: (i // bpb, 0, 0)),
            pl.BlockSpec((tm, LANES), lambda i, dr: (i, 0)),
            pl.BlockSpec(memory_space=pl.ANY),
        ],
        out_specs=pl.BlockSpec((tm, D_MODEL), lambda i, dr: (i, 0)),
        scratch_shapes=[
            pltpu.VMEM((2, tm * SUB, LANES), F32),
            pltpu.SemaphoreType.DMA(()),
        ],
    )
    return pl.pallas_call(
        _combine_body,
        grid_spec=grid_spec,
        out_shape=jax.ShapeDtypeStruct((rows, D_MODEL), F32),
        compiler_params=_params(("arbitrary",), disable_bounds_checks=True),
        name="combine",
    )(dest_flat, y, g2, w, y2)


def _moe(y, gain, sh, sc, g2, wrh, wrl, wg, wu, wd, rows_per_batch, tm, tr):
    rows = y.shape[0]
    ntile = 2 * rows // tr + N_EXPERTS
    slots, w, cnt = _route(y, gain, sh, sc, wrh, wrl, rows_per_batch, tm)
    dest_flat, te, nused, ztiles = _tile_plan(cnt[0, :N_EXPERTS], slots[:, :4], ntile, tr)
    xs = _dispatch(dest_flat, ztiles, y, gain, sh, sc, rows_per_batch, tm, ntile, tr)
    y2 = _experts(te, nused, xs, wg, wu, wd, tr)
    return _combine(dest_flat, y, g2, w, y2, rows_per_batch, tm)


def _rope_tables(n_tok, dim):
    rows = n_tok // GRID_W
    row = jnp.repeat(jnp.arange(rows), GRID_W)
    col = jnp.tile(jnp.arange(GRID_W), rows)
    half = dim // 2
    freqs = ROPE_BASE ** (-jnp.arange(0, half, 2, dtype=F32) / half)

    def ang(p):
        a = p.astype(F32)[:, None] * freqs[None, :]
        return jnp.concatenate([a, a], axis=-1)

    angles = jnp.concatenate([ang(row), ang(col)], axis=-1)
    cos, sin = jnp.cos(angles), jnp.sin(angles)
    first = (np.arange(dim) % half) < (half // 2)
    sa = jnp.where(first, -sin, 0.0)
    sb = jnp.where(first, 0.0, sin)
    pad = LANES - dim
    cos = jnp.pad(cos, ((0, 0), (0, pad)), constant_values=1.0)
    sa = jnp.pad(sa, ((0, 0), (0, pad)))
    sb = jnp.pad(sb, ((0, 0), (0, pad)))
    return cos, sa, sb


def _pack_w_in(w_in):
    zeros = jnp.zeros(w_in.shape[:-1] + (LANES - QK_HEAD_DIM,), w_in.dtype)
    parts = []
    for hd in range(MLA_HEADS):
        b = hd * QK_HEAD_DIM
        parts += [w_in[..., b + QK_NOPE_DIM:b + QK_HEAD_DIM], w_in[..., b:b + QK_NOPE_DIM], zeros]
    o = MLA_HEADS * QK_HEAD_DIM
    parts.append(w_in[..., o:o + KV_LORA_RANK])
    o += KV_LORA_RANK
    parts += [w_in[..., o:o + QK_ROPE_DIM],
              jnp.zeros(w_in.shape[:-1] + (LANES - QK_ROPE_DIM,), w_in.dtype)]
    o += QK_ROPE_DIM
    parts += [w_in[..., o:o + RET_WIDTH], w_in[..., o + 2 * RET_WIDTH:]]
    w_kret_t = jnp.swapaxes(w_in[..., o + RET_WIDTH:o + 2 * RET_WIDTH], -1, -2)
    return jnp.concatenate(parts, axis=-1).astype(BF16), w_kret_t.astype(BF16)


def _pack_w_ukv(w_ukv):
    lead = w_ukv.shape[:-1]
    z32 = jnp.zeros(lead + (QK_ROPE_DIM,), w_ukv.dtype)
    z64 = jnp.zeros(lead + (V_HEAD_DIM,), w_ukv.dtype)
    kp, vp = [], []
    for hd in range(MLA_HEADS):
        b = hd * (QK_NOPE_DIM + V_HEAD_DIM)
        kp += [z32, w_ukv[..., b:b + QK_NOPE_DIM], z32]
        v = w_ukv[..., b + QK_NOPE_DIM:b + QK_NOPE_DIM + V_HEAD_DIM]
        vp += [v, z64] if hd % 2 == 0 else [z64, v]
    return jnp.concatenate(kp, axis=-1).astype(BF16), jnp.concatenate(vp, axis=-1).astype(BF16)


def _pack_head_gain(g):
    z = jnp.zeros(g.shape[:-1] + (LANES - QK_HEAD_DIM,), g.dtype)
    return jnp.concatenate([g[..., QK_NOPE_DIM:], g[..., :QK_NOPE_DIM], z], axis=-1)


def kernel(x_prompt, x_sample, cache_ckv, cache_krope, state_ret, c, c_ctx, attn_norm, ffn_norm, w_ada, b_ada, w_in, kv_norm, w_ukv, q_norm, k_norm, decay_logit, w_out, w_ffn_gate, w_ffn_up, w_ffn_down, w_router, w_exp_gate, w_exp_up, w_exp_down):
    nb_c, seq_c, _ = x_prompt.shape
    nb_l, seq_l, _ = x_sample.shape
    tm = ROW_BLOCK

    wp, wkt = _pack_w_in(w_in)
    wk, wv = _pack_w_ukv(w_ukv)
    wo = w_out.astype(BF16)
    wfg, wfu, wfd = w_ffn_gate.astype(BF16), w_ffn_up.astype(BF16), w_ffn_down.astype(BF16)
    weg, weu, wed = w_exp_gate.astype(BF16), w_exp_up.astype(BF16), w_exp_down.astype(BF16)
    wr = jnp.pad(w_router, ((0, 0), (0, 0), (0, LANES - N_EXPERTS)))
    wrh = wr.astype(BF16)
    wrl = (wr - wrh.astype(F32)).astype(BF16)
    qg = _pack_head_gain(q_norm)[:, None, :]
    kg = _pack_head_gain(k_norm)[:, None, :]
    kvg = kv_norm[:, None, :]
    lg = jax.nn.log_sigmoid(decay_logit.astype(F32))
    tabs_m = _rope_tables(seq_l, QK_ROPE_DIM)
    tabs_r = _rope_tables(seq_l, RET_DK)
    tabs_rt = tuple(t.T for t in tabs_r)
    cache_kr_pad = jnp.pad(cache_krope, ((0, 0), (0, 0), (0, 0), (0, LANES - QK_ROPE_DIM)))

    cs = jnp.concatenate([c_ctx[None, :], c, jnp.zeros((SUB - 1 - nb_l, D_MODEL), F32)], axis=0)
    mods = _adaln(cs, w_ada, b_ada).reshape(DEPTH, SUB, 6, 1, D_MODEL)

    yp = x_prompt.reshape(nb_c * seq_c, D_MODEL)
    ys = x_sample.reshape(nb_l * seq_l, D_MODEL)
    ckv_list, krope_list, state_list = [], [], []
    for l in range(DEPTH):
        an, fn = attn_norm[l][None, :], ffn_norm[l][None, :]
        for ctx in (True, False):
            if ctx:
                y, m, rpb = yp, mods[l, 0:1], nb_c * seq_c
            else:
                y, m, rpb = ys, mods[l, 1:1 + nb_l], seq_l
            sh1, sc1, g1, sh2, sc2, g2 = (m[:, j] for j in range(6))
            q, ckv, kr, qr, kret, vr, g = _inproj(
                y, an, sh1, sc1, wp[l], wkt[l], qg[l], kvg[l],
                None if ctx else tabs_m + tabs_r + tabs_rt, rpb, tm)
            if ctx:
                a = _attn_ctx(q, ckv, kr, wk[l], wv[l], kg[l], seq_c)
                r, st = _retention(lg[l], qr, kret, vr, g, None, seq_c, RET_CTX_SEQS)
                ckv_list.append(ckv.reshape(nb_c, seq_c, KV_LORA_RANK))
                krope_list.append(kr[:, :QK_ROPE_DIM].reshape(nb_c, seq_c, QK_ROPE_DIM))
                state_list.append(st)
            else:
                a = _attn_lat(q, ckv, kr, cache_ckv[:, l], cache_kr_pad[:, l], wk[l], wv[l], kg[l],
                              tabs_m, seq_l, Q_BLOCK)
                r, = _retention(lg[l], qr, kret, vr, g, state_ret[:, l], seq_l, 1)
            y = _outproj(y, a, r, wo[l], g1, rpb, tm)
            i = l // 2
            if l % 2 == 0:
                y = _ffn(y, fn, sh2, sc2, g2, wfg[i], wfu[i], wfd[i], rpb, tm)
            else:
                y = _moe(y, fn, sh2, sc2, g2, wrh[i], wrl[i], weg[i], weu[i], wed[i], rpb, tm, EXPERT_TILE)
            if ctx:
                yp = y
            else:
                ys = y
    return (yp.reshape(nb_c, seq_c, D_MODEL), ys.reshape(nb_l, seq_l, D_MODEL),
            jnp.stack(ckv_list, axis=1), jnp.stack(krope_list, axis=1), jnp.stack(state_list, axis=1))
```

```python
import functools
import math

import jax
import jax.numpy as jnp
import numpy as np
from jax import lax
from jax.experimental import pallas as pl
from jax.experimental.pallas import tpu as pltpu

D_MODEL = 1024
DEPTH = 4
GRID_W = 64
MLA_HEADS = 8
QK_NOPE_DIM = 64
QK_ROPE_DIM = 32
QK_HEAD_DIM = QK_NOPE_DIM + QK_ROPE_DIM
V_HEAD_DIM = 64
KV_LORA_RANK = 256
MLA_WIDTH = MLA_HEADS * V_HEAD_DIM
RET_HEADS = 4
RET_DK = 128
RET_DV = 128
RET_CHUNK = 128
RET_WIDTH = RET_HEADS * RET_DV
D_FF = 2816
N_EXPERTS = 8
EXPERT_FF = 1408
ROPE_BASE = 10000.0
EPS = 1e-6

LANES = 128
SUB = 8
QP_WIDTH = MLA_HEADS * LANES
OFF_Q = 0
OFF_CKV = OFF_Q + QP_WIDTH
OFF_KR = OFF_CKV + KV_LORA_RANK
OFF_QR = OFF_KR + LANES
OFF_VR = OFF_QR + RET_WIDTH
OFF_G = OFF_VR + RET_WIDTH
IN_COLS_P = OFF_G + RET_WIDTH

VMEM_LIMIT = 56 * 1024 * 1024
LOG2E = math.log2(math.e)

ROW_BLOCK = 512
Q_BLOCK = 256
EXPERT_TILE = 256
RET_CTX_SEQS = 4

F32 = jnp.float32
BF16 = jnp.bfloat16


def _dot(a, b):
    return jnp.dot(a, b, preferred_element_type=F32)


def _dot_nt(a, b):
    return lax.dot_general(a, b, (((1,), (1,)), ((), ())), preferred_element_type=F32)


def _dot_tn(a, b):
    return lax.dot_general(a, b, (((0,), (0,)), ((), ())), preferred_element_type=F32)


def _rms(x, n):
    return x * lax.rsqrt(jnp.sum(x * x, axis=-1, keepdims=True) * (1.0 / n) + EPS)


def _silu(x):
    return x * (1.0 / (1.0 + jnp.exp(-x)))


def _rope(t, cos, sa, sb, q):
    return t * cos + pltpu.roll(t, LANES - q, 1) * sa + pltpu.roll(t, q, 1) * sb


def _params(sem, **kw):
    return pltpu.CompilerParams(dimension_semantics=sem, vmem_limit_bytes=VMEM_LIMIT, **kw)


def _layer(arr, idx, **kw):
    nd = arr.ndim
    return pl.BlockSpec((None,) + arr.shape[1:], lambda *g: (idx,) + (0,) * (nd - 1), **kw)


class _Stream:
    def __init__(self, mods, layer, base, blocks_per_entry):
        self.mods, self.layer, self.base, self.bpe = mods, layer, base, blocks_per_entry

    def mod(self, which):
        l, base, bpe = self.layer, self.base, self.bpe
        return pl.BlockSpec((None, None, None, 1, D_MODEL), lambda i, *_: (l, base + i // bpe, which, 0, 0))


def _adaln_body(c_ref, w_ref, b_ref, o_ref):
    x = _silu(c_ref[...]).astype(BF16)
    o_ref[0] = _dot(x, w_ref[0].astype(BF16)) + b_ref[0]


def _adaln(cs, w_ada, b_ada):
    tn = 1536
    nb = cs.shape[0]
    return pl.pallas_call(
        _adaln_body,
        grid=(DEPTH, 6 * D_MODEL // tn),
        in_specs=[
            pl.BlockSpec((nb, D_MODEL), lambda l, j: (0, 0)),
            pl.BlockSpec((1, D_MODEL, tn), lambda l, j: (l, 0, j)),
            pl.BlockSpec((1, 1, tn), lambda l, j: (l, 0, j)),
        ],
        out_specs=pl.BlockSpec((1, nb, tn), lambda l, j: (l, 0, j)),
        out_shape=jax.ShapeDtypeStruct((DEPTH, nb, 6 * D_MODEL), F32),
        compiler_params=_params(("arbitrary", "arbitrary")),
        name="adaln",
    )(cs, w_ada, b_ada.reshape(DEPTH, 1, 6 * D_MODEL))


def _inproj_body(rope, y_ref, gain_ref, sh_ref, sc_ref, w_ref, wkt_ref, qg_ref, kvg_ref, *rest):
    if rope:
        cm_ref, sam_ref, sbm_ref, cr_ref, sar_ref, sbr_ref, crt_ref, sart_ref, sbrt_ref = rest[:9]
        rest = rest[9:]
    q_ref, ckv_ref, kr_ref, qr_ref, kt_ref, vr_ref, g_ref = rest
    tm = y_ref.shape[0]

    x = y_ref[...]
    h = _rms(x, D_MODEL) * gain_ref[...]
    h = h * (1.0 + sc_ref[...]) + sh_ref[...]
    hb = h.astype(BF16)

    pq = _dot(hb, w_ref[:, OFF_Q:OFF_Q + QP_WIDTH])
    pckv = _dot(hb, w_ref[:, OFF_CKV:OFF_CKV + KV_LORA_RANK])
    kr_ref[...] = _dot(hb, w_ref[:, OFF_KR:OFF_KR + LANES])
    pqr = _dot(hb, w_ref[:, OFF_QR:OFF_QR + RET_WIDTH])
    kt = _dot_nt(wkt_ref[...], hb)
    vr_ref[...] = _dot(hb, w_ref[:, OFF_VR:OFF_VR + RET_WIDTH]).astype(BF16)
    g_ref[...] = _dot(hb, w_ref[:, OFF_G:OFF_G + RET_WIDTH]).astype(BF16)

    qg = qg_ref[...]
    for hd in range(MLA_HEADS):
        t = _rms(pq[:, hd * LANES:(hd + 1) * LANES], QK_HEAD_DIM) * qg
        if rope:
            t = _rope(t, cm_ref[...], sam_ref[...], sbm_ref[...], QK_ROPE_DIM // 4)
        q_ref[:, hd * LANES:(hd + 1) * LANES] = t.astype(BF16)

    ckv_ref[...] = _rms(pckv, KV_LORA_RANK) * kvg_ref[...]

    for hd in range(RET_HEADS):
        c0 = hd * LANES
        t = pqr[:, c0:c0 + LANES]
        if rope:
            t = _rope(t, cr_ref[...], sar_ref[...], sbr_ref[...], RET_DK // 4)
        qr_ref[:, c0:c0 + LANES] = t.astype(BF16)
    qtr = RET_DK // 4
    for hd in range(RET_HEADS):
        t = kt[hd * RET_DK:(hd + 1) * RET_DK, :]
        if rope:
            up = jnp.concatenate([t[qtr:], t[:qtr]], axis=0)
            down = jnp.concatenate([t[RET_DK - qtr:], t[:RET_DK - qtr]], axis=0)
            t = t * crt_ref[...] + up * sart_ref[...] + down * sbrt_ref[...]
        t = (t * (RET_DK ** -0.5)).astype(BF16)
        for j in range(tm // RET_CHUNK):
            kt_ref[j, hd * RET_DK:(hd + 1) * RET_DK, :] = t[:, j * RET_CHUNK:(j + 1) * RET_CHUNK]


def _inproj(y, st, gain, wp, wkt, qg, kvg, tables, tm):
    rows = y.shape[0]
    rope = tables is not None
    l, bpb = st.layer, st.bpe
    cpb = tm // RET_CHUNK
    in_specs = [
        pl.BlockSpec((tm, D_MODEL), lambda i: (i, 0)),
        _layer(gain, l), st.mod(0), st.mod(1), _layer(wp, l), _layer(wkt, l), _layer(qg, l), _layer(kvg, l),
    ]
    args = [y, gain, st.mods, st.mods, wp, wkt, qg, kvg]
    if rope:
        in_specs += [pl.BlockSpec((tm, LANES), lambda i: (i % bpb, 0))] * 6
        in_specs += [pl.BlockSpec((RET_DK, tm), lambda i: (0, i % bpb))] * 3
        args += list(tables)
    blocks = [(tm, QP_WIDTH), (tm, KV_LORA_RANK), (tm, LANES), (tm, RET_WIDTH),
              (cpb, RET_WIDTH, RET_CHUNK), (tm, RET_WIDTH), (tm, RET_WIDTH)]
    dtypes = (BF16, F32, F32, BF16, BF16, BF16, BF16)
    out_specs, out_shape = [], []
    for blk, dt in zip(blocks, dtypes):
        nd = len(blk)
        out_specs.append(pl.BlockSpec(blk, lambda i, nd=nd: (i,) + (0,) * (nd - 1)))
        out_shape.append(jax.ShapeDtypeStruct((rows // tm * blk[0],) + blk[1:], dt))
    return pl.pallas_call(
        functools.partial(_inproj_body, rope),
        grid=(rows // tm,),
        in_specs=in_specs,
        out_specs=out_specs,
        out_shape=out_shape,
        compiler_params=_params(("arbitrary",)),
        name="inproj_rope" if rope else "inproj",
    )(*args)


def _make_kv(ckv_f32, kr_pad, wk_ref, wv_ref, kg, rope_tabs):
    cb = ckv_f32.astype(BF16)
    kall = _dot(cb, wk_ref[...])
    vall = _dot(cb, wv_ref[...])
    ks, vs = [], []
    for hd in range(MLA_HEADS):
        k = kall[:, hd * LANES:(hd + 1) * LANES] + kr_pad
        k = _rms(k, QK_HEAD_DIM) * kg
        if rope_tabs is not None:
            k = _rope(k, *rope_tabs, QK_ROPE_DIM // 4)
        ks.append(k.astype(BF16))
        vs.append(vall[:, hd * LANES:(hd + 1) * LANES].astype(BF16))
    return ks, vs


def _softmax_pv(s, v):
    m = jnp.max(s, axis=-1, keepdims=True)
    p = jnp.exp2((s - m) * (QK_HEAD_DIM ** -0.5 * LOG2E))
    l = jnp.sum(p, axis=-1, keepdims=True)
    return _dot(p.astype(BF16), v) * (1.0 / l)


def _attend(q_ref, get_k, get_v, o_ref):
    def scores(hd):
        return _dot_nt(q_ref[:, hd * LANES:(hd + 1) * LANES], get_k(hd))

    s_next = scores(0)
    acc = None
    for hd in range(MLA_HEADS):
        s = s_next
        if hd + 1 < MLA_HEADS:
            s_next = scores(hd + 1)
        o = _softmax_pv(s, get_v(hd))
        acc = o if hd % 2 == 0 else acc + o
        if hd % 2 == 1:
            o_ref[:, (hd // 2) * LANES:(hd // 2 + 1) * LANES] = acc.astype(BF16)


def _attn_ctx_body(q_ref, ckv_ref, kr_ref, wk_ref, wv_ref, kg_ref, o_ref):
    ks, vs = _make_kv(ckv_ref[...], kr_ref[...], wk_ref, wv_ref, kg_ref[...], None)
    _attend(q_ref, lambda hd: ks[hd], lambda hd: vs[hd], o_ref)


def _attn_ctx(q, ckv, kr, wk, wv, kg, l, seq):
    rows = q.shape[0]
    row = lambda b: (b, 0)
    return pl.pallas_call(
        _attn_ctx_body,
        grid=(rows // seq,),
        in_specs=[
            pl.BlockSpec((seq, QP_WIDTH), row),
            pl.BlockSpec((seq, KV_LORA_RANK), row),
            pl.BlockSpec((seq, LANES), row),
            _layer(wk, l), _layer(wv, l), _layer(kg, l),
        ],
        out_specs=pl.BlockSpec((seq, MLA_WIDTH), row),
        out_shape=jax.ShapeDtypeStruct((rows, MLA_WIDTH), BF16),
        compiler_params=_params(("arbitrary",)),
        name="attn_ctx",
    )(q, ckv, kr, wk, wv, kg)


def _attn_lat_body(past, seq, tq, q_ref, ckv_ref, kr_ref, cckv_ref, ckr_ref, wk_ref, wv_ref, kg_ref,
                   cm_ref, sam_ref, sbm_ref, o_ref, k_s, v_s):
    kc = 256

    @pl.when(pl.program_id(1) == 0)
    def _():
        kg = kg_ref[...]
        for c in range(past // kc):
            r = slice(c * kc, (c + 1) * kc)
            ks, vs = _make_kv(cckv_ref[r, :], ckr_ref[r, :], wk_ref, wv_ref, kg, None)
            for hd in range(MLA_HEADS):
                k_s[hd, r, :] = ks[hd]
                v_s[hd, r, :] = vs[hd]

        def chunk(c, carry):
            r0 = pl.multiple_of(c * kc, kc)
            rs = pl.ds(r0, kc)
            tabs = (cm_ref[rs, :], sam_ref[rs, :], sbm_ref[rs, :])
            ks, vs = _make_kv(ckv_ref[rs, :], kr_ref[rs, :], wk_ref, wv_ref, kg, tabs)
            ro = pl.ds(past + r0, kc)
            for hd in range(MLA_HEADS):
                k_s[hd, ro, :] = ks[hd]
                v_s[hd, ro, :] = vs[hd]
            return carry

        lax.fori_loop(0, seq // kc, chunk, 0)

    _attend(q_ref, lambda hd: k_s[hd], lambda hd: v_s[hd], o_ref)


def _attn_lat(q, ckv, kr, cache_ckv, cache_kr, wk, wv, kg, l, tabs_m, seq, tq):
    rows = q.shape[0]
    nb = rows // seq
    nq = seq // tq
    past = cache_ckv.shape[2]
    const = lambda b, i: (0, 0)
    return pl.pallas_call(
        functools.partial(_attn_lat_body, past, seq, tq),
        grid=(nb, nq),
        in_specs=[
            pl.BlockSpec((tq, QP_WIDTH), lambda b, i: (b * nq + i, 0)),
            pl.BlockSpec((seq, KV_LORA_RANK), lambda b, i: (b, 0)),
            pl.BlockSpec((seq, LANES), lambda b, i: (b, 0)),
            pl.BlockSpec((None, None, past, KV_LORA_RANK), lambda b, i: (b, l, 0, 0)),
            pl.BlockSpec((None, None, past, LANES), lambda b, i: (b, l, 0, 0)),
            _layer(wk, l), _layer(wv, l), _layer(kg, l),
            pl.BlockSpec((seq, LANES), const),
            pl.BlockSpec((seq, LANES), const),
            pl.BlockSpec((seq, LANES), const),
        ],
        out_specs=pl.BlockSpec((tq, MLA_WIDTH), lambda b, i: (b * nq + i, 0)),
        out_shape=jax.ShapeDtypeStruct((rows, MLA_WIDTH), BF16),
        scratch_shapes=[
            pltpu.VMEM((MLA_HEADS, past + seq, LANES), BF16),
            pltpu.VMEM((MLA_HEADS, past + seq, LANES), BF16),
        ],
        compiler_params=_params(("arbitrary", "arbitrary")),
        name="attn_lat",
    )(q, ckv, kr, cache_ckv, cache_kr, wk, wv, kg, *tabs_m)


def _ret_body(seq, nseq, has_state, layer, lg_ref, q_ref, kt_ref, v_ref, g_ref, *rest):
    if has_state:
        s0_ref, o_ref, acc_s, u_s, st_s, dec_s = rest
    else:
        o_ref, stout_ref, acc_s, u_s, st_s, dec_s = rest
    C = RET_CHUNK
    nc = seq // C
    ii = lax.broadcasted_iota(jnp.int32, (C, C), 0)
    jj = lax.broadcasted_iota(jnp.int32, (C, C), 1)
    diff = (ii - jj).astype(F32)
    pos = ii.astype(F32)
    post = jj.astype(F32)
    cfull = jnp.full((1, LANES), float(C), F32)
    cdf, cdb = [], []
    for hd in range(RET_HEADS):
        lgf = lg_ref[layer, 0, hd]
        lgb = lg_ref[layer, 1, hd]
        dec_s[hd, 0] = (jnp.where(diff >= 0, jnp.exp(lgf * jnp.maximum(diff, 0.0)), 0.0)
                        + jnp.where(diff <= 0, jnp.exp(lgb * jnp.maximum(-diff, 0.0)), 0.0))
        dec_s[hd, 1] = jnp.exp(lgf * (pos + 1.0))
        dec_s[hd, 2] = jnp.exp(lgb * (C - pos))
        dec_s[hd, 3] = jnp.exp(lgf * (C - 1.0 - post))
        dec_s[hd, 4] = jnp.exp(lgb * post)
        cdf.append(jnp.exp(lgf * cfull))
        cdb.append(jnp.exp(lgb * cfull))

    def p1(c, carry):
        rs = pl.ds(pl.multiple_of(c * C, C), C)
        heads = range(RET_HEADS)
        cols = [slice(hd * LANES, (hd + 1) * LANES) for hd in heads]
        kts = [kt_ref[c, hd * RET_DK:(hd + 1) * RET_DK, :] for hd in heads]
        vcs = [v_ref[rs, cols[hd]] for hd in heads]
        scores = [_dot(q_ref[rs, cols[hd]], kts[hd]) for hd in heads]
        for hd in heads:
            kf = kts[hd].astype(F32)
            kd = jnp.concatenate([(kf * dec_s[hd, 3]).astype(BF16), (kf * dec_s[hd, 4]).astype(BF16)], axis=0)
            u_s[c, hd] = _dot(kd, vcs[hd])
        for hd in heads:
            acc_s[rs, cols[hd]] = _dot((scores[hd] * dec_s[hd, 0]).astype(BF16), vcs[hd])
        return carry

    lax.fori_loop(0, nseq * nc, p1, 0)

    for sq in range(nseq):
        for hd in range(RET_HEADS):
            if has_state:
                init = (s0_ref[sq, 0, hd], s0_ref[sq, 1, hd])
            else:
                init = (jnp.zeros((RET_DK, RET_DV), F32), jnp.zeros((RET_DK, RET_DV), F32))

            def p2(t, carry, sq=sq, hd=hd):
                sf, sb = carry
                cf = sq * nc + t
                cb = sq * nc + (nc - 1 - t)
                st_s[cf, hd, :, 0:RET_DV] = sf.astype(BF16)
                st_s[cb, hd, :, RET_DV:2 * RET_DV] = sb.astype(BF16)
                sf = sf * cdf[hd] + u_s[cf, hd, 0:RET_DK, :]
                sb = sb * cdb[hd] + u_s[cb, hd, RET_DK:2 * RET_DK, :]
                return sf, sb

            sf, sb = lax.fori_loop(0, nc, p2, init)
            if not has_state:
                stout_ref[sq, 0, hd] = sf
                stout_ref[sq, 1, hd] = sb

    def p3(c, carry):
        rs = pl.ds(pl.multiple_of(c * C, C), C)
        crs = [_dot(q_ref[rs, hd * LANES:(hd + 1) * LANES], st_s[c, hd]) for hd in range(RET_HEADS)]
        for hd in range(RET_HEADS):
            cols = slice(hd * LANES, (hd + 1) * LANES)
            cr = crs[hd]
            r = acc_s[rs, cols] + cr[:, 0:RET_DV] * dec_s[hd, 1] + cr[:, RET_DV:2 * RET_DV] * dec_s[hd, 2]
            gh = g_ref[rs, cols].astype(F32)
            o_ref[rs, cols] = (_rms(r, RET_DV) * _silu(gh)).astype(BF16)
        return carry

    lax.fori_loop(0, nseq * nc, p3, 0)


def _retention(lg, l, qr, kt, vr, g, s0, seq, nseq):
    rows = qr.shape[0]
    nb = rows // seq
    has_state = s0 is not None
    nct = nseq * seq // RET_CHUNK
    row = lambda b: (b, 0)
    if has_state:
        st_spec = pl.BlockSpec((nseq, None, 2, RET_HEADS, RET_DK, RET_DV), lambda b: (b, l, 0, 0, 0, 0))
    else:
        st_spec = pl.BlockSpec((nseq, 2, RET_HEADS, RET_DK, RET_DV), lambda b: (b, 0, 0, 0, 0))
    tok = pl.BlockSpec((nseq * seq, RET_WIDTH), row)
    in_specs = [pl.BlockSpec(memory_space=pltpu.SMEM), tok,
                pl.BlockSpec((nct, RET_WIDTH, RET_CHUNK), lambda b: (b, 0, 0)), tok, tok]
    args = [lg, qr, kt, vr, g]
    out_specs = [pl.BlockSpec((nseq * seq, RET_WIDTH), row)]
    out_shape = [jax.ShapeDtypeStruct((rows, RET_WIDTH), BF16)]
    if has_state:
        in_specs.append(st_spec)
        args.append(s0)
    else:
        out_specs.append(st_spec)
        out_shape.append(jax.ShapeDtypeStruct((nb, 2, RET_HEADS, RET_DK, RET_DV), F32))
    return pl.pallas_call(
        functools.partial(_ret_body, seq, nseq, has_state, l),
        grid=(nb // nseq,),
        in_specs=in_specs,
        out_specs=out_specs,
        out_shape=out_shape,
        scratch_shapes=[
            pltpu.VMEM((nseq * seq, RET_WIDTH), F32),
            pltpu.VMEM((nct, RET_HEADS, 2 * RET_DK, RET_DV), F32),
            pltpu.VMEM((nct, RET_HEADS, RET_DK, 2 * RET_DV), BF16),
            pltpu.VMEM((RET_HEADS, 5, RET_CHUNK, RET_CHUNK), F32),
        ],
        compiler_params=_params(("arbitrary",)),
        name="retention_lat" if has_state else "retention_ctx",
    )(*args)


def _mixer_residual(y_ref, a_ref, r_ref, wo_ref, g1_ref):
    out = _dot(a_ref[...], wo_ref[:MLA_WIDTH, :]) + _dot(r_ref[...], wo_ref[MLA_WIDTH:, :])
    return y_ref[...] + g1_ref[...] * out


def _ffn_in(y, gain_ref, sh_ref, sc_ref):
    h = _rms(y, D_MODEL) * gain_ref[...]
    return h * (1.0 + sc_ref[...]) + sh_ref[...]


def _ffn_body(y_ref, a_ref, r_ref, wo_ref, g1_ref, gain_ref, sh_ref, sc_ref, g2_ref, wg_ref, wu_ref, wd_ref, o_ref):
    y = _mixer_residual(y_ref, a_ref, r_ref, wo_ref, g1_ref)
    hb = _ffn_in(y, gain_ref, sh_ref, sc_ref).astype(BF16)
    a = (_silu(_dot(hb, wg_ref[...])) * _dot(hb, wu_ref[...])).astype(BF16)
    o_ref[...] = y + g2_ref[...] * _dot(a, wd_ref[...])


def _ffn(y, a, r, st, wo, gain, wg, wu, wd, i, tm):
    rows = y.shape[0]
    l = st.layer
    row = lambda b: (b, 0)
    once = pl.Buffered(1)
    return pl.pallas_call(
        _ffn_body,
        grid=(rows // tm,),
        in_specs=[
            pl.BlockSpec((tm, D_MODEL), row),
            pl.BlockSpec((tm, MLA_WIDTH), row),
            pl.BlockSpec((tm, RET_WIDTH), row),
            _layer(wo, l, pipeline_mode=once), st.mod(2),
            _layer(gain, l), st.mod(3), st.mod(4), st.mod(5),
            _layer(wg, i, pipeline_mode=once), _layer(wu, i, pipeline_mode=once), _layer(wd, i, pipeline_mode=once),
        ],
        out_specs=pl.BlockSpec((tm, D_MODEL), row),
        out_shape=jax.ShapeDtypeStruct((rows, D_MODEL), F32),
        compiler_params=_params(("arbitrary",)),
        name="ffn",
    )(y, a, r, wo, st.mods, gain, st.mods, st.mods, st.mods, wg, wu, wd)


def _split_bf16(x):
    hi = x.astype(BF16)
    return hi, (x - hi.astype(F32)).astype(BF16)


def _route_body(y_ref, a_ref, r_ref, wo_ref, g1_ref, gain_ref, sh_ref, sc_ref, wrh_ref, wrl_ref,
                y1_ref, slot_ref, w_ref, cnt_ref, carry):
    tm = y_ref.shape[0]

    @pl.when(pl.program_id(0) == 0)
    def _():
        carry[...] = jnp.zeros_like(carry)

    y = _mixer_residual(y_ref, a_ref, r_ref, wo_ref, g1_ref)
    y1_ref[...] = y
    h = _ffn_in(y, gain_ref, sh_ref, sc_ref)
    hi, lo = _split_bf16(h)
    logits = _dot(hi, wrh_ref[...]) + (_dot(lo, wrh_ref[...]) + _dot(hi, wrl_ref[...]))
    lane = lax.broadcasted_iota(jnp.int32, logits.shape, 1)
    ninf = jnp.float32(-jnp.inf)
    lg = jnp.where(lane < N_EXPERTS, logits, ninf)
    m1 = jnp.max(lg, axis=-1, keepdims=True)
    i1 = jnp.min(jnp.where(lg == m1, lane, LANES), axis=-1, keepdims=True)
    lg2 = jnp.where(lane == i1, ninf, lg)
    m2 = jnp.max(lg2, axis=-1, keepdims=True)
    i2 = jnp.min(jnp.where(lg2 == m2, lane, LANES), axis=-1, keepdims=True)
    ex = jnp.exp(m2 - m1)
    den = 1.0 / (1.0 + ex)
    w_ref[...] = jnp.where(lane == 0, den, 0.0) + jnp.where(lane == 1, ex * den, 0.0)

    sel1 = lane == i1
    sel2 = lane == i2
    onehot = jnp.where(sel1 | sel2, 1.0, 0.0)
    ri = lax.broadcasted_iota(jnp.int32, (tm, tm), 0)
    ci = lax.broadcasted_iota(jnp.int32, (tm, tm), 1)
    lower = jnp.where(ri > ci, 1.0, 0.0).astype(BF16)
    before = _dot(lower, onehot.astype(BF16)) + carry[...]
    r1 = jnp.sum(jnp.where(sel1, before, 0.0), axis=-1, keepdims=True)
    r2 = jnp.sum(jnp.where(sel2, before, 0.0), axis=-1, keepdims=True)
    slot_ref[...] = (jnp.where(lane == 0, i1, 0) + jnp.where(lane == 1, i2, 0)
                     + jnp.where(lane == 2, r1.astype(jnp.int32), 0)
                     + jnp.where(lane == 3, r2.astype(jnp.int32), 0))
    carry[...] += jnp.sum(onehot, axis=0, keepdims=True)
    cnt_ref[...] = carry[...].astype(jnp.int32)


def _route(y, a, r, st, wo, gain, wrh, wrl, i, tm):
    rows = y.shape[0]
    l = st.layer
    row = lambda b: (b, 0)
    return pl.pallas_call(
        _route_body,
        grid=(rows // tm,),
        in_specs=[
            pl.BlockSpec((tm, D_MODEL), row),
            pl.BlockSpec((tm, MLA_WIDTH), row),
            pl.BlockSpec((tm, RET_WIDTH), row),
            _layer(wo, l), st.mod(2),
            _layer(gain, l), st.mod(3), st.mod(4),
            _layer(wrh, i), _layer(wrl, i),
        ],
        out_specs=[pl.BlockSpec((tm, D_MODEL), row), pl.BlockSpec((tm, LANES), row),
                   pl.BlockSpec((tm, LANES), row), pl.BlockSpec((1, LANES), lambda b: (0, 0))],
        out_shape=[jax.ShapeDtypeStruct((rows, D_MODEL), F32), jax.ShapeDtypeStruct((rows, LANES), jnp.int32),
                   jax.ShapeDtypeStruct((rows, LANES), F32), jax.ShapeDtypeStruct((1, LANES), jnp.int32)],
        scratch_shapes=[pltpu.VMEM((1, LANES), F32)],
        compiler_params=_params(("arbitrary",)),
        name="route",
    )(y, a, r, wo, st.mods, gain, st.mods, st.mods, wrh, wrl)


def _dispatch_body(tr, dest_ref, zt_ref, y_ref, gain_ref, sh_ref, sc_ref, xs_hbm, stage, zeros, sem, zsem):
    i = pl.program_id(0)
    tm = y_ref.shape[0]

    @pl.when(i == 0)
    def _():
        zeros[...] = jnp.zeros_like(zeros)

        def ztile(z):
            r0 = pl.multiple_of(zt_ref[z] * (tr * SUB), tr * SUB)
            return pltpu.make_async_copy(zeros, xs_hbm.at[pl.ds(r0, tr * SUB)], zsem)

        for z in range(zt_ref.shape[0]):
            @pl.when(zt_ref[z] >= 0)
            def _():
                ztile(z).start()
        for z in range(zt_ref.shape[0]):
            @pl.when(zt_ref[z] >= 0)
            def _():
                ztile(z).wait()

    h = _ffn_in(y_ref[...], gain_ref, sh_ref, sc_ref)
    for s in range(D_MODEL // LANES):
        stage[pl.ds(s, tm, stride=SUB), :] = h[:, s * LANES:(s + 1) * LANES]

    def body(t, c):
        src = stage.at[pl.ds(pl.multiple_of(t * SUB, SUB), SUB)]
        for k in range(2):
            dst = dest_ref[(i * tm + t) * 2 + k]
            pltpu.make_async_copy(src, xs_hbm.at[pl.ds(pl.multiple_of(dst * SUB, SUB), SUB)],
                                  sem).start(priority=k)
        return c

    lax.fori_loop(0, tm, body, 0, unroll=8)

    for k in range(2):
        pltpu.make_async_copy(stage, stage, sem).wait()


def _dispatch(dest_flat, ztiles, y, st, gain, tm, ntile, tr):
    rows = y.shape[0]
    grid_spec = pltpu.PrefetchScalarGridSpec(
        num_scalar_prefetch=2,
        grid=(rows // tm,),
        in_specs=[
            pl.BlockSpec((tm, D_MODEL), lambda i, dr, cr: (i, 0)),
            _layer(gain, st.layer), st.mod(3), st.mod(4),
        ],
        out_specs=pl.BlockSpec(memory_space=pl.ANY),
        scratch_shapes=[
            pltpu.VMEM((tm * SUB, LANES), F32),
            pltpu.VMEM((tr * SUB, LANES), F32),
            pltpu.SemaphoreType.DMA(()),
            pltpu.SemaphoreType.DMA(()),
        ],
    )
    return pl.pallas_call(
        functools.partial(_dispatch_body, tr),
        grid_spec=grid_spec,
        out_shape=jax.ShapeDtypeStruct((ntile * tr * SUB, LANES), F32),
        compiler_params=_params(("arbitrary",), disable_bounds_checks=True),
        name="dispatch",
    )(dest_flat, ztiles, y, gain, st.mods, st.mods)


def _experts_body(tr, te_ref, nu_ref, xs_ref, wg_ref, wu_ref, wd_ref, o_ref):
    nt = D_MODEL // LANES
    used = pl.program_id(0) < nu_ref[0]

    @pl.when(used)
    def _():
        x = jnp.concatenate([xs_ref[pl.ds(s, tr, stride=SUB), :] for s in range(nt)], axis=1).astype(BF16)
        a = (_silu(_dot(x, wg_ref[0])) * _dot(x, wu_ref[0])).astype(BF16)
        o = _dot(a, wd_ref[0])
        for s in range(nt):
            o_ref[pl.ds(s, tr, stride=SUB), :] = o[:, s * LANES:(s + 1) * LANES]

    @pl.when(jnp.logical_not(used))
    def _():
        o_ref[...] = jnp.zeros_like(o_ref)


def _experts(tile_e, nused, xs, wg, wu, wd, i, tr):
    ntile = tile_e.shape[0]
    blk = lambda j, te, nu: (j, 0)
    wsel = lambda j, te, nu: (i, te[j], 0, 0)
    grid_spec = pltpu.PrefetchScalarGridSpec(
        num_scalar_prefetch=2,
        grid=(ntile,),
        in_specs=[
            pl.BlockSpec((tr * SUB, LANES), blk),
            pl.BlockSpec((None, 1, D_MODEL, EXPERT_FF), wsel),
            pl.BlockSpec((None, 1, D_MODEL, EXPERT_FF), wsel),
            pl.BlockSpec((None, 1, EXPERT_FF, D_MODEL), wsel),
        ],
        out_specs=pl.BlockSpec((tr * SUB, LANES), blk),
    )
    return pl.pallas_call(
        functools.partial(_experts_body, tr),
        grid_spec=grid_spec,
        out_shape=jax.ShapeDtypeStruct(xs.shape, F32),
        compiler_params=_params(("arbitrary",)),
        name="experts",
    )(tile_e, nused, xs, wg, wu, wd)


def _tile_plan(cnt, slots, ntile, tr):
    ptiles = (cnt + tr - 1) // tr
    tile_end = jnp.cumsum(ptiles)
    nused = tile_end[-1]
    off = (tile_end - ptiles) * tr
    experts = jnp.arange(N_EXPERTS, dtype=jnp.int32)
    dest = slots[:, 2:4] + jnp.sum(jnp.where(slots[:, 0:2, None] == experts, off, 0), axis=-1)
    j = jnp.minimum(jnp.arange(ntile), nused - 1)
    te = jnp.sum((j[:, None] >= tile_end[None, :]).astype(jnp.int32), axis=1)
    last = jnp.where(ptiles > 0, tile_end - 1, -1)
    tail = nused + jnp.arange(ntile - slots.shape[0] * 2 // tr)
    ztiles = jnp.concatenate([last, jnp.where(tail < ntile, tail, -1)])
    i32 = lambda a: a.astype(jnp.int32)
    return i32(dest).reshape(-1), i32(te), i32(nused).reshape(1), i32(ztiles)


def _combine_body(dest_ref, y_ref, g2_ref, w_ref, y2_hbm, o_ref, stage, sem):
    i = pl.program_id(0)
    tm = y_ref.shape[0]

    def body(t, c):
        for k in range(2):
            src = dest_ref[(i * tm + t) * 2 + k]
            pltpu.make_async_copy(y2_hbm.at[pl.ds(pl.multiple_of(src * SUB, SUB), SUB)],
                                  stage.at[k, pl.ds(pl.multiple_of(t * SUB, SUB), SUB)],
                                  sem).start(priority=k)
        return c

    lax.fori_loop(0, tm, body, 0, unroll=8)
    w = w_ref[...]
    w1 = w[:, 0:1]
    w2 = w[:, 1:2]
    for k in range(2):
        pltpu.make_async_copy(stage.at[k], stage.at[k], sem).wait()
    parts = []
    for s in range(D_MODEL // LANES):
        parts.append(w1 * stage[0, pl.ds(s, tm, stride=SUB), :] + w2 * stage[1, pl.ds(s, tm, stride=SUB), :])
    o_ref[...] = y_ref[...] + g2_ref[...] * jnp.concatenate(parts, axis=1)


def _combine(dest_flat, y, st, w, y2, tm):
    rows = y.shape[0]
    grid_spec = pltpu.PrefetchScalarGridSpec(
        num_scalar_prefetch=1,
        grid=(rows // tm,),
        in_specs=[
            pl.BlockSpec((tm, D_MODEL), lambda i, dr: (i, 0)),
            st.mod(5),
            pl.BlockSpec((tm, LANES), lambda i, dr: (i, 0)),
            pl.BlockSpec(memory_space=pl.ANY),
        ],
        out_specs=pl.BlockSpec((tm, D_MODEL), lambda i, dr: (i, 0)),
        scratch_shapes=[
            pltpu.VMEM((2, tm * SUB, LANES), F32),
            pltpu.SemaphoreType.DMA(()),
        ],
    )
    return pl.pallas_call(
        _combine_body,
        grid_spec=grid_spec,
        out_shape=jax.ShapeDtypeStruct((rows, D_MODEL), F32),
        compiler_params=_params(("arbitrary",), disable_bounds_checks=True),
        name="combine",
    )(dest_flat, y, st.mods, w, y2)


def _moe(y, a, r, st, wo, gain, wrh, wrl, wg, wu, wd, i, tm, tr):
    rows = y.shape[0]
    ntile = 2 * rows // tr + N_EXPERTS
    y1, slots, w, cnt = _route(y, a, r, st, wo, gain, wrh, wrl, i, tm)
    dest_flat, te, nused, ztiles = _tile_plan(cnt[0, :N_EXPERTS], slots[:, :4], ntile, tr)
    xs = _dispatch(dest_flat, ztiles, y1, st, gain, tm, ntile, tr)
    y2 = _experts(te, nused, xs, wg, wu, wd, i, tr)
    return _combine(dest_flat, y1, st, w, y2, tm)


def _rope_tables(n_tok, dim):
    rows = n_tok // GRID_W
    row = jnp.repeat(jnp.arange(rows), GRID_W)
    col = jnp.tile(jnp.arange(GRID_W), rows)
    half = dim // 2
    freqs = ROPE_BASE ** (-jnp.arange(0, half, 2, dtype=F32) / half)

    def ang(p):
        a = p.astype(F32)[:, None] * freqs[None, :]
        return jnp.concatenate([a, a], axis=-1)

    angles = jnp.concatenate([ang(row), ang(col)], axis=-1)
    cos, sin = jnp.cos(angles), jnp.sin(angles)
    first = (np.arange(dim) % half) < (half // 2)
    sa = jnp.where(first, -sin, 0.0)
    sb = jnp.where(first, 0.0, sin)
    pad = LANES - dim
    cos = jnp.pad(cos, ((0, 0), (0, pad)), constant_values=1.0)
    sa = jnp.pad(sa, ((0, 0), (0, pad)))
    sb = jnp.pad(sb, ((0, 0), (0, pad)))
    return cos, sa, sb


def _pack_w_in(w_in):
    zeros = jnp.zeros(w_in.shape[:-1] + (LANES - QK_HEAD_DIM,), w_in.dtype)
    parts = []
    for hd in range(MLA_HEADS):
        b = hd * QK_HEAD_DIM
        parts += [w_in[..., b + QK_NOPE_DIM:b + QK_HEAD_DIM], w_in[..., b:b + QK_NOPE_DIM], zeros]
    o = MLA_HEADS * QK_HEAD_DIM
    parts.append(w_in[..., o:o + KV_LORA_RANK])
    o += KV_LORA_RANK
    parts += [w_in[..., o:o + QK_ROPE_DIM],
              jnp.zeros(w_in.shape[:-1] + (LANES - QK_ROPE_DIM,), w_in.dtype)]
    o += QK_ROPE_DIM
    parts += [w_in[..., o:o + RET_WIDTH], w_in[..., o + 2 * RET_WIDTH:]]
    w_kret_t = jnp.swapaxes(w_in[..., o + RET_WIDTH:o + 2 * RET_WIDTH], -1, -2)
    return jnp.concatenate(parts, axis=-1).astype(BF16), w_kret_t.astype(BF16)


def _pack_w_ukv(w_ukv):
    lead = w_ukv.shape[:-1]
    z32 = jnp.zeros(lead + (QK_ROPE_DIM,), w_ukv.dtype)
    z64 = jnp.zeros(lead + (V_HEAD_DIM,), w_ukv.dtype)
    kp, vp = [], []
    for hd in range(MLA_HEADS):
        b = hd * (QK_NOPE_DIM + V_HEAD_DIM)
        kp += [z32, w_ukv[..., b:b + QK_NOPE_DIM], z32]
        v = w_ukv[..., b + QK_NOPE_DIM:b + QK_NOPE_DIM + V_HEAD_DIM]
        vp += [v, z64] if hd % 2 == 0 else [z64, v]
    return jnp.concatenate(kp, axis=-1).astype(BF16), jnp.concatenate(vp, axis=-1).astype(BF16)


def _pack_head_gain(g):
    z = jnp.zeros(g.shape[:-1] + (LANES - QK_HEAD_DIM,), g.dtype)
    return jnp.concatenate([g[..., QK_NOPE_DIM:], g[..., :QK_NOPE_DIM], z], axis=-1)


def kernel(x_prompt, x_sample, cache_ckv, cache_krope, state_ret, c, c_ctx, attn_norm, ffn_norm, w_ada, b_ada, w_in, kv_norm, w_ukv, q_norm, k_norm, decay_logit, w_out, w_ffn_gate, w_ffn_up, w_ffn_down, w_router, w_exp_gate, w_exp_up, w_exp_down):
    nb_c, seq_c, _ = x_prompt.shape
    nb_l, seq_l, _ = x_sample.shape
    tm = ROW_BLOCK

    wp, wkt = _pack_w_in(w_in)
    wk, wv = _pack_w_ukv(w_ukv)
    wo = w_out.astype(BF16)
    wfg, wfu, wfd = w_ffn_gate.astype(BF16), w_ffn_up.astype(BF16), w_ffn_down.astype(BF16)
    weg, weu, wed = w_exp_gate.astype(BF16), w_exp_up.astype(BF16), w_exp_down.astype(BF16)
    wr = jnp.pad(w_router, ((0, 0), (0, 0), (0, LANES - N_EXPERTS)))
    wrh = wr.astype(BF16)
    wrl = (wr - wrh.astype(F32)).astype(BF16)
    qg = _pack_head_gain(q_norm)[:, None, :]
    kg = _pack_head_gain(k_norm)[:, None, :]
    kvg = kv_norm[:, None, :]
    lg = jax.nn.log_sigmoid(decay_logit.astype(F32))
    tabs_m = _rope_tables(seq_l, QK_ROPE_DIM)
    tabs_r = _rope_tables(seq_l, RET_DK)
    tabs_rt = tuple(t.T for t in tabs_r)
    cache_kr_pad = jnp.pad(cache_krope, ((0, 0), (0, 0), (0, 0), (0, LANES - QK_ROPE_DIM)))

    cs = jnp.concatenate([c_ctx[None, :], c, jnp.zeros((SUB - 1 - nb_l, D_MODEL), F32)], axis=0)
    mods = _adaln(cs, w_ada, b_ada).reshape(DEPTH, SUB, 6, 1, D_MODEL)

    yp = x_prompt.reshape(nb_c * seq_c, D_MODEL)
    ys = x_sample.reshape(nb_l * seq_l, D_MODEL)
    ckv_list, krope_list, state_list = [], [], []
    an, fn = attn_norm[:, None, :], ffn_norm[:, None, :]
    for l in range(DEPTH):
        for ctx in (True, False):
            if ctx:
                y, st = yp, _Stream(mods, l, 0, nb_c * seq_c // tm)
            else:
                y, st = ys, _Stream(mods, l, 1, seq_l // tm)
            q, ckv, kr, qr, kret, vr, g = _inproj(
                y, st, an, wp, wkt, qg, kvg, None if ctx else tabs_m + tabs_r + tabs_rt, tm)
            if ctx:
                a = _attn_ctx(q, ckv, kr, wk, wv, kg, l, seq_c)
                r, state = _retention(lg, l, qr, kret, vr, g, None, seq_c, RET_CTX_SEQS)
                ckv_list.append(ckv.reshape(nb_c, seq_c, KV_LORA_RANK))
                krope_list.append(kr[:, :QK_ROPE_DIM].reshape(nb_c, seq_c, QK_ROPE_DIM))
                state_list.append(state)
            else:
                a = _attn_lat(q, ckv, kr, cache_ckv, cache_kr_pad, wk, wv, kg, l, tabs_m, seq_l, Q_BLOCK)
                r, = _retention(lg, l, qr, kret, vr, g, state_ret, seq_l, 1)
            i = l // 2
            if l % 2 == 0:
                y = _ffn(y, a, r, st, wo, fn, wfg, wfu, wfd, i, tm)
            else:
                y = _moe(y, a, r, st, wo, fn, wrh, wrl, weg, weu, wed, i, tm, EXPERT_TILE)
            if ctx:
                yp = y
            else:
                ys = y
    return (yp.reshape(nb_c, seq_c, D_MODEL), ys.reshape(nb_l, seq_l, D_MODEL),
            jnp.stack(ckv_list, axis=1), jnp.stack(krope_list, axis=1), jnp.stack(state_list, axis=1))
```

```python
import functools
import math

import jax
import jax.numpy as jnp
import numpy as np
from jax import lax
from jax.experimental import pallas as pl
from jax.experimental.pallas import tpu as pltpu

D_MODEL = 1024
DEPTH = 4
GRID_W = 64
MLA_HEADS = 8
QK_NOPE_DIM = 64
QK_ROPE_DIM = 32
QK_HEAD_DIM = QK_NOPE_DIM + QK_ROPE_DIM
V_HEAD_DIM = 64
KV_LORA_RANK = 256
MLA_WIDTH = MLA_HEADS * V_HEAD_DIM
RET_HEADS = 4
RET_DK = 128
RET_DV = 128
RET_CHUNK = 128
RET_WIDTH = RET_HEADS * RET_DV
D_FF = 2816
N_EXPERTS = 8
EXPERT_FF = 1408
ROPE_BASE = 10000.0
EPS = 1e-6

LANES = 128
SUB = 8
QP_WIDTH = MLA_HEADS * LANES
OFF_Q = 0
OFF_CKV = OFF_Q + QP_WIDTH
OFF_KR = OFF_CKV + KV_LORA_RANK
OFF_QR = OFF_KR + LANES
OFF_VR = OFF_QR + RET_WIDTH
OFF_G = OFF_VR + RET_WIDTH
IN_COLS_P = OFF_G + RET_WIDTH

VMEM_LIMIT = 56 * 1024 * 1024
LOG2E = math.log2(math.e)

ROW_BLOCK = 512
Q_BLOCK = 512
EXPERT_TILE = 256
RET_CTX_SEQS = 4

F32 = jnp.float32
BF16 = jnp.bfloat16


def _dot(a, b):
    return jnp.dot(a, b, preferred_element_type=F32)


def _dot_nt(a, b):
    return lax.dot_general(a, b, (((1,), (1,)), ((), ())), preferred_element_type=F32)


def _dot_tn(a, b):
    return lax.dot_general(a, b, (((0,), (0,)), ((), ())), preferred_element_type=F32)


def _rms(x, n):
    return x * lax.rsqrt(jnp.sum(x * x, axis=-1, keepdims=True) * (1.0 / n) + EPS)


def _silu(x):
    return x * (1.0 / (1.0 + jnp.exp(-x)))


def _rope(t, cos, sa, sb, q):
    return t * cos + pltpu.roll(t, LANES - q, 1) * sa + pltpu.roll(t, q, 1) * sb


def _rope_t(t, cos, sa, sb, q):
    n = t.shape[0]
    up = jnp.concatenate([t[q:], t[:q]], axis=0)
    down = jnp.concatenate([t[n - q:], t[:n - q]], axis=0)
    return t * cos + up * sa + down * sb


def _params(sem, **kw):
    return pltpu.CompilerParams(dimension_semantics=sem, vmem_limit_bytes=VMEM_LIMIT, **kw)


def _layer(arr, idx, **kw):
    nd = arr.ndim
    return pl.BlockSpec((None,) + arr.shape[1:], lambda *g: (idx,) + (0,) * (nd - 1), **kw)


class _Stream:
    def __init__(self, mods, layer, base, blocks_per_entry):
        self.mods, self.layer, self.base, self.bpe = mods, layer, base, blocks_per_entry

    def mod(self, which):
        l, base, bpe = self.layer, self.base, self.bpe
        return pl.BlockSpec((None, None, None, 1, D_MODEL), lambda i, *_: (l, base + i // bpe, which, 0, 0))


def _adaln_body(c_ref, w_ref, b_ref, o_ref):
    x = _silu(c_ref[...]).astype(BF16)
    o_ref[0] = _dot(x, w_ref[0].astype(BF16)) + b_ref[0]


def _adaln(cs, w_ada, b_ada):
    tn = 1536
    nb = cs.shape[0]
    return pl.pallas_call(
        _adaln_body,
        grid=(DEPTH, 6 * D_MODEL // tn),
        in_specs=[
            pl.BlockSpec((nb, D_MODEL), lambda l, j: (0, 0)),
            pl.BlockSpec((1, D_MODEL, tn), lambda l, j: (l, 0, j)),
            pl.BlockSpec((1, 1, tn), lambda l, j: (l, 0, j)),
        ],
        out_specs=pl.BlockSpec((1, nb, tn), lambda l, j: (l, 0, j)),
        out_shape=jax.ShapeDtypeStruct((DEPTH, nb, 6 * D_MODEL), F32),
        compiler_params=_params(("arbitrary", "arbitrary")),
        name="adaln",
    )(cs, w_ada, b_ada.reshape(DEPTH, 1, 6 * D_MODEL))


def _inproj_body(rope, y_ref, gain_ref, sh_ref, sc_ref, w_ref, wkt_ref, qg_ref, kvg_ref, *rest):
    if rope:
        cm_ref, sam_ref, sbm_ref, cr_ref, sar_ref, sbr_ref, crt_ref, sart_ref, sbrt_ref = rest[:9]
        rest = rest[9:]
    q_ref, ckv_ref, kr_ref, qr_ref, kt_ref, vr_ref, g_ref = rest
    tm = y_ref.shape[0]

    x = y_ref[...]
    h = _rms(x, D_MODEL) * gain_ref[...]
    h = h * (1.0 + sc_ref[...]) + sh_ref[...]
    hb = h.astype(BF16)

    pq = _dot(hb, w_ref[:, OFF_Q:OFF_Q + QP_WIDTH])
    pckv = _dot(hb, w_ref[:, OFF_CKV:OFF_CKV + KV_LORA_RANK])
    kr_ref[...] = _dot(hb, w_ref[:, OFF_KR:OFF_KR + LANES])
    pqr = _dot(hb, w_ref[:, OFF_QR:OFF_QR + RET_WIDTH])
    kt = _dot_nt(wkt_ref[...], hb)
    vr_ref[...] = _dot(hb, w_ref[:, OFF_VR:OFF_VR + RET_WIDTH]).astype(BF16)
    g_ref[...] = _dot(hb, w_ref[:, OFF_G:OFF_G + RET_WIDTH]).astype(BF16)

    qg = qg_ref[...]
    for hd in range(MLA_HEADS):
        t = _rms(pq[:, hd * LANES:(hd + 1) * LANES], QK_HEAD_DIM) * qg
        if rope:
            t = _rope(t, cm_ref[...], sam_ref[...], sbm_ref[...], QK_ROPE_DIM // 4)
        q_ref[:, hd * LANES:(hd + 1) * LANES] = t.astype(BF16)

    ckv_ref[...] = _rms(pckv, KV_LORA_RANK) * kvg_ref[...]

    for hd in range(RET_HEADS):
        c0 = hd * LANES
        t = pqr[:, c0:c0 + LANES]
        if rope:
            t = _rope(t, cr_ref[...], sar_ref[...], sbr_ref[...], RET_DK // 4)
        qr_ref[:, c0:c0 + LANES] = t.astype(BF16)
    for hd in range(RET_HEADS):
        t = kt[hd * RET_DK:(hd + 1) * RET_DK, :]
        if rope:
            t = _rope_t(t, crt_ref[...], sart_ref[...], sbrt_ref[...], RET_DK // 4)
        t = (t * (RET_DK ** -0.5)).astype(BF16)
        for j in range(tm // RET_CHUNK):
            kt_ref[j, hd * RET_DK:(hd + 1) * RET_DK, :] = t[:, j * RET_CHUNK:(j + 1) * RET_CHUNK]


def _inproj(y, st, gain, wp, wkt, qg, kvg, tables, tm):
    rows = y.shape[0]
    rope = tables is not None
    l, bpb = st.layer, st.bpe
    cpb = tm // RET_CHUNK
    in_specs = [
        pl.BlockSpec((tm, D_MODEL), lambda i: (i, 0)),
        _layer(gain, l), st.mod(0), st.mod(1), _layer(wp, l), _layer(wkt, l), _layer(qg, l), _layer(kvg, l),
    ]
    args = [y, gain, st.mods, st.mods, wp, wkt, qg, kvg]
    if rope:
        in_specs += [pl.BlockSpec((tm, LANES), lambda i: (i % bpb, 0))] * 6
        in_specs += [pl.BlockSpec((RET_DK, tm), lambda i: (0, i % bpb))] * 3
        args += list(tables)
    blocks = [(tm, QP_WIDTH), (tm, KV_LORA_RANK), (tm, LANES), (tm, RET_WIDTH),
              (cpb, RET_WIDTH, RET_CHUNK), (tm, RET_WIDTH), (tm, RET_WIDTH)]
    dtypes = (BF16, F32, F32, BF16, BF16, BF16, BF16)
    out_specs, out_shape = [], []
    for blk, dt in zip(blocks, dtypes):
        nd = len(blk)
        out_specs.append(pl.BlockSpec(blk, lambda i, nd=nd: (i,) + (0,) * (nd - 1)))
        out_shape.append(jax.ShapeDtypeStruct((rows // tm * blk[0],) + blk[1:], dt))
    return pl.pallas_call(
        functools.partial(_inproj_body, rope),
        grid=(rows // tm,),
        in_specs=in_specs,
        out_specs=out_specs,
        out_shape=out_shape,
        compiler_params=_params(("arbitrary",)),
        name="inproj_rope" if rope else "inproj",
    )(*args)


def _make_kv(ckv_f32, kr_pad, wk_ref, wv_ref, kg, rope_tabs):
    cb = ckv_f32.astype(BF16)
    kall = _dot(cb, wk_ref[...])
    vall = _dot(cb, wv_ref[...])
    ks, vs = [], []
    for hd in range(MLA_HEADS):
        k = kall[:, hd * LANES:(hd + 1) * LANES] + kr_pad
        k = _rms(k, QK_HEAD_DIM) * kg
        if rope_tabs is not None:
            k = _rope(k, *rope_tabs, QK_ROPE_DIM // 4)
        ks.append(k.astype(BF16))
        vs.append(vall[:, hd * LANES:(hd + 1) * LANES].astype(BF16))
    return ks, vs


def _softmax_pv(s, v):
    m = jnp.max(s, axis=-1, keepdims=True)
    p = jnp.exp2((s - m) * (QK_HEAD_DIM ** -0.5 * LOG2E))
    l = jnp.sum(p, axis=-1, keepdims=True)
    return _dot(p.astype(BF16), v) * (1.0 / l)


def _attend(q_ref, get_k, get_v, o_ref):
    def scores(hd):
        return _dot_nt(q_ref[:, hd * LANES:(hd + 1) * LANES], get_k(hd))

    s_next = scores(0)
    acc = None
    for hd in range(MLA_HEADS):
        s = s_next
        if hd + 1 < MLA_HEADS:
            s_next = scores(hd + 1)
        o = _softmax_pv(s, get_v(hd))
        acc = o if hd % 2 == 0 else acc + o
        if hd % 2 == 1:
            o_ref[:, (hd // 2) * LANES:(hd // 2 + 1) * LANES] = acc.astype(BF16)


def _attn_ctx_body(q_ref, ckv_ref, kr_ref, wk_ref, wv_ref, kg_ref, o_ref):
    ks, vs = _make_kv(ckv_ref[...], kr_ref[...], wk_ref, wv_ref, kg_ref[...], None)
    _attend(q_ref, lambda hd: ks[hd], lambda hd: vs[hd], o_ref)


def _attn_ctx(q, ckv, kr, wk, wv, kg, l, seq):
    rows = q.shape[0]
    row = lambda b: (b, 0)
    return pl.pallas_call(
        _attn_ctx_body,
        grid=(rows // seq,),
        in_specs=[
            pl.BlockSpec((seq, QP_WIDTH), row),
            pl.BlockSpec((seq, KV_LORA_RANK), row),
            pl.BlockSpec((seq, LANES), row),
            _layer(wk, l), _layer(wv, l), _layer(kg, l),
        ],
        out_specs=pl.BlockSpec((seq, MLA_WIDTH), row),
        out_shape=jax.ShapeDtypeStruct((rows, MLA_WIDTH), BF16),
        compiler_params=_params(("arbitrary",)),
        name="attn_ctx",
    )(q, ckv, kr, wk, wv, kg)


def _attn_lat_body(past, seq, tq, q_ref, ckv_ref, kr_ref, cckv_ref, ckr_ref, wk_ref, wv_ref, kg_ref,
                   cm_ref, sam_ref, sbm_ref, o_ref, k_s, v_s):
    kc = 256

    @pl.when(pl.program_id(1) == 0)
    def _():
        kg = kg_ref[...]
        for c in range(past // kc):
            r = slice(c * kc, (c + 1) * kc)
            ks, vs = _make_kv(cckv_ref[r, :], ckr_ref[r, :], wk_ref, wv_ref, kg, None)
            for hd in range(MLA_HEADS):
                k_s[hd, r, :] = ks[hd]
                v_s[hd, r, :] = vs[hd]

        def chunk(c, carry):
            r0 = pl.multiple_of(c * kc, kc)
            rs = pl.ds(r0, kc)
            tabs = (cm_ref[rs, :], sam_ref[rs, :], sbm_ref[rs, :])
            ks, vs = _make_kv(ckv_ref[rs, :], kr_ref[rs, :], wk_ref, wv_ref, kg, tabs)
            ro = pl.ds(past + r0, kc)
            for hd in range(MLA_HEADS):
                k_s[hd, ro, :] = ks[hd]
                v_s[hd, ro, :] = vs[hd]
            return carry

        lax.fori_loop(0, seq // kc, chunk, 0)

    _attend(q_ref, lambda hd: k_s[hd], lambda hd: v_s[hd], o_ref)


def _attn_lat(q, ckv, kr, cache_ckv, cache_kr, wk, wv, kg, l, tabs_m, seq, tq):
    rows = q.shape[0]
    nb = rows // seq
    nq = seq // tq
    past = cache_ckv.shape[2]
    const = lambda b, i: (0, 0)
    return pl.pallas_call(
        functools.partial(_attn_lat_body, past, seq, tq),
        grid=(nb, nq),
        in_specs=[
            pl.BlockSpec((tq, QP_WIDTH), lambda b, i: (b * nq + i, 0)),
            pl.BlockSpec((seq, KV_LORA_RANK), lambda b, i: (b, 0)),
            pl.BlockSpec((seq, LANES), lambda b, i: (b, 0)),
            pl.BlockSpec((None, None, past, KV_LORA_RANK), lambda b, i: (b, l, 0, 0)),
            pl.BlockSpec((None, None, past, LANES), lambda b, i: (b, l, 0, 0)),
            _layer(wk, l), _layer(wv, l), _layer(kg, l),
            pl.BlockSpec((seq, LANES), const),
            pl.BlockSpec((seq, LANES), const),
            pl.BlockSpec((seq, LANES), const),
        ],
        out_specs=pl.BlockSpec((tq, MLA_WIDTH), lambda b, i: (b * nq + i, 0)),
        out_shape=jax.ShapeDtypeStruct((rows, MLA_WIDTH), BF16),
        scratch_shapes=[
            pltpu.VMEM((MLA_HEADS, past + seq, LANES), BF16),
            pltpu.VMEM((MLA_HEADS, past + seq, LANES), BF16),
        ],
        compiler_params=_params(("arbitrary", "arbitrary")),
        name="attn_lat",
    )(q, ckv, kr, cache_ckv, cache_kr, wk, wv, kg, *tabs_m)


def _ret_body(seq, nseq, has_state, layer, lg_ref, q_ref, kt_ref, v_ref, g_ref, *rest):
    if has_state:
        s0_ref, o_ref, acc_s, u_s, st_s, dec_s = rest
    else:
        o_ref, stout_ref, acc_s, u_s, st_s, dec_s = rest
    C = RET_CHUNK
    nc = seq // C
    ii = lax.broadcasted_iota(jnp.int32, (C, C), 0)
    jj = lax.broadcasted_iota(jnp.int32, (C, C), 1)
    diff = (ii - jj).astype(F32)
    pos = ii.astype(F32)
    post = jj.astype(F32)
    cfull = jnp.full((1, LANES), float(C), F32)
    cdf, cdb = [], []
    for hd in range(RET_HEADS):
        lgf = lg_ref[layer, 0, hd]
        lgb = lg_ref[layer, 1, hd]
        dec_s[hd, 0] = (jnp.where(diff >= 0, jnp.exp(lgf * jnp.maximum(diff, 0.0)), 0.0)
                        + jnp.where(diff <= 0, jnp.exp(lgb * jnp.maximum(-diff, 0.0)), 0.0))
        dec_s[hd, 1] = jnp.exp(lgf * (pos + 1.0))
        dec_s[hd, 2] = jnp.exp(lgb * (C - pos))
        dec_s[hd, 3] = jnp.exp(lgf * (C - 1.0 - post))
        dec_s[hd, 4] = jnp.exp(lgb * post)
        cdf.append(jnp.exp(lgf * cfull))
        cdb.append(jnp.exp(lgb * cfull))

    per_trip = 2
    items = [(sub, hd) for sub in range(per_trip) for hd in range(RET_HEADS)]
    cols = [slice(hd * LANES, (hd + 1) * LANES) for hd in range(RET_HEADS)]

    def p1(trip, carry):
        cs = [trip * per_trip + sub for sub in range(per_trip)]
        rs = [pl.ds(pl.multiple_of(c * C, C), C) for c in cs]
        kts = {(s, h): kt_ref[cs[s], h * RET_DK:(h + 1) * RET_DK, :] for s, h in items}
        vcs = {(s, h): v_ref[rs[s], cols[h]] for s, h in items}
        scores = {(s, h): _dot(q_ref[rs[s], cols[h]], kts[s, h]) for s, h in items}
        for s, h in items:
            kf = kts[s, h].astype(F32)
            kd = jnp.concatenate([(kf * dec_s[h, 3]).astype(BF16), (kf * dec_s[h, 4]).astype(BF16)], axis=0)
            u_s[cs[s], h] = _dot(kd, vcs[s, h])
        for s, h in items:
            acc_s[rs[s], cols[h]] = _dot((scores[s, h] * dec_s[h, 0]).astype(BF16), vcs[s, h])
        return carry

    lax.fori_loop(0, nseq * nc // per_trip, p1, 0)

    for sq in range(nseq):
        for hd in range(RET_HEADS):
            if has_state:
                init = (s0_ref[sq, 0, hd], s0_ref[sq, 1, hd])
            else:
                init = (jnp.zeros((RET_DK, RET_DV), F32), jnp.zeros((RET_DK, RET_DV), F32))

            def p2(t, carry, sq=sq, hd=hd):
                sf, sb = carry
                cf = sq * nc + t
                cb = sq * nc + (nc - 1 - t)
                st_s[cf, hd, :, 0:RET_DV] = sf.astype(BF16)
                st_s[cb, hd, :, RET_DV:2 * RET_DV] = sb.astype(BF16)
                sf = sf * cdf[hd] + u_s[cf, hd, 0:RET_DK, :]
                sb = sb * cdb[hd] + u_s[cb, hd, RET_DK:2 * RET_DK, :]
                return sf, sb

            sf, sb = lax.fori_loop(0, nc, p2, init)
            if not has_state:
                stout_ref[sq, 0, hd] = sf
                stout_ref[sq, 1, hd] = sb

    def p3(trip, carry):
        cs = [trip * per_trip + sub for sub in range(per_trip)]
        rs = [pl.ds(pl.multiple_of(c * C, C), C) for c in cs]
        crs = {(s, h): _dot(q_ref[rs[s], cols[h]], st_s[cs[s], h]) for s, h in items}
        for s, h in items:
            cr = crs[s, h]
            r = acc_s[rs[s], cols[h]] + cr[:, 0:RET_DV] * dec_s[h, 1] + cr[:, RET_DV:2 * RET_DV] * dec_s[h, 2]
            gh = g_ref[rs[s], cols[h]].astype(F32)
            o_ref[rs[s], cols[h]] = (_rms(r, RET_DV) * _silu(gh)).astype(BF16)
        return carry

    lax.fori_loop(0, nseq * nc // per_trip, p3, 0)


def _retention(lg, l, qr, kt, vr, g, s0, seq, nseq):
    rows = qr.shape[0]
    nb = rows // seq
    has_state = s0 is not None
    nct = nseq * seq // RET_CHUNK
    row = lambda b: (b, 0)
    if has_state:
        st_spec = pl.BlockSpec((nseq, None, 2, RET_HEADS, RET_DK, RET_DV), lambda b: (b, l, 0, 0, 0, 0))
    else:
        st_spec = pl.BlockSpec((nseq, 2, RET_HEADS, RET_DK, RET_DV), lambda b: (b, 0, 0, 0, 0))
    tok = pl.BlockSpec((nseq * seq, RET_WIDTH), row)
    in_specs = [pl.BlockSpec(memory_space=pltpu.SMEM), tok,
                pl.BlockSpec((nct, RET_WIDTH, RET_CHUNK), lambda b: (b, 0, 0)), tok, tok]
    args = [lg, qr, kt, vr, g]
    out_specs = [pl.BlockSpec((nseq * seq, RET_WIDTH), row)]
    out_shape = [jax.ShapeDtypeStruct((rows, RET_WIDTH), BF16)]
    if has_state:
        in_specs.append(st_spec)
        args.append(s0)
    else:
        out_specs.append(st_spec)
        out_shape.append(jax.ShapeDtypeStruct((nb, 2, RET_HEADS, RET_DK, RET_DV), F32))
    return pl.pallas_call(
        functools.partial(_ret_body, seq, nseq, has_state, l),
        grid=(nb // nseq,),
        in_specs=in_specs,
        out_specs=out_specs,
        out_shape=out_shape,
        scratch_shapes=[
            pltpu.VMEM((nseq * seq, RET_WIDTH), F32),
            pltpu.VMEM((nct, RET_HEADS, 2 * RET_DK, RET_DV), F32),
            pltpu.VMEM((nct, RET_HEADS, RET_DK, 2 * RET_DV), BF16),
            pltpu.VMEM((RET_HEADS, 5, RET_CHUNK, RET_CHUNK), F32),
        ],
        compiler_params=_params(("arbitrary",)),
        name="retention_lat" if has_state else "retention_ctx",
    )(*args)


def _mixer_residual(y_ref, a_ref, r_ref, wo_ref, g1_ref):
    out = _dot(a_ref[...], wo_ref[:MLA_WIDTH, :]) + _dot(r_ref[...], wo_ref[MLA_WIDTH:, :])
    return y_ref[...] + g1_ref[...] * out


def _ffn_in(y, gain_ref, sh_ref, sc_ref):
    h = _rms(y, D_MODEL) * gain_ref[...]
    return h * (1.0 + sc_ref[...]) + sh_ref[...]


def _ffn_body(y_ref, a_ref, r_ref, wo_ref, g1_ref, gain_ref, sh_ref, sc_ref, g2_ref, wg_ref, wu_ref, wd_ref, o_ref):
    y = _mixer_residual(y_ref, a_ref, r_ref, wo_ref, g1_ref)
    hb = _ffn_in(y, gain_ref, sh_ref, sc_ref).astype(BF16)
    a = (_silu(_dot(hb, wg_ref[...])) * _dot(hb, wu_ref[...])).astype(BF16)
    o_ref[...] = y + g2_ref[...] * _dot(a, wd_ref[...])


def _ffn(y, a, r, st, wo, gain, wg, wu, wd, i, tm):
    rows = y.shape[0]
    l = st.layer
    row = lambda b: (b, 0)
    once = pl.Buffered(1)
    return pl.pallas_call(
        _ffn_body,
        grid=(rows // tm,),
        in_specs=[
            pl.BlockSpec((tm, D_MODEL), row),
            pl.BlockSpec((tm, MLA_WIDTH), row),
            pl.BlockSpec((tm, RET_WIDTH), row),
            _layer(wo, l, pipeline_mode=once), st.mod(2),
            _layer(gain, l), st.mod(3), st.mod(4), st.mod(5),
            _layer(wg, i, pipeline_mode=once), _layer(wu, i, pipeline_mode=once), _layer(wd, i, pipeline_mode=once),
        ],
        out_specs=pl.BlockSpec((tm, D_MODEL), row),
        out_shape=jax.ShapeDtypeStruct((rows, D_MODEL), F32),
        compiler_params=_params(("arbitrary",)),
        name="ffn",
    )(y, a, r, wo, st.mods, gain, st.mods, st.mods, st.mods, wg, wu, wd)


def _split_bf16(x):
    hi = x.astype(BF16)
    return hi, (x - hi.astype(F32)).astype(BF16)


def _route_body(y_ref, a_ref, r_ref, wo_ref, g1_ref, gain_ref, sh_ref, sc_ref, wr_ref,
                y1_ref, slot_ref, w_ref, cnt_ref, carry):
    tm = y_ref.shape[0]

    @pl.when(pl.program_id(0) == 0)
    def _():
        carry[...] = jnp.zeros_like(carry)

    y = _mixer_residual(y_ref, a_ref, r_ref, wo_ref, g1_ref)
    y1_ref[...] = y
    h = _ffn_in(y, gain_ref, sh_ref, sc_ref)
    hi, lo = _split_bf16(h)
    hh_hl = _dot(hi, wr_ref[...])
    logits = hh_hl[:, :LANES] + (_dot(lo, wr_ref[:, :LANES]) + hh_hl[:, LANES:])
    lane = lax.broadcasted_iota(jnp.int32, logits.shape, 1)
    ninf = jnp.float32(-jnp.inf)
    lg = jnp.where(lane < N_EXPERTS, logits, ninf)
    m1 = jnp.max(lg, axis=-1, keepdims=True)
    i1 = jnp.min(jnp.where(lg == m1, lane, LANES), axis=-1, keepdims=True)
    lg2 = jnp.where(lane == i1, ninf, lg)
    m2 = jnp.max(lg2, axis=-1, keepdims=True)
    i2 = jnp.min(jnp.where(lg2 == m2, lane, LANES), axis=-1, keepdims=True)
    ex = jnp.exp(m2 - m1)
    den = 1.0 / (1.0 + ex)
    w_ref[...] = jnp.where(lane == 0, den, 0.0) + jnp.where(lane == 1, ex * den, 0.0)

    sel1 = lane == i1
    sel2 = lane == i2
    onehot = jnp.where(sel1 | sel2, 1.0, 0.0)
    ri = lax.broadcasted_iota(jnp.int32, (tm, tm), 0)
    ci = lax.broadcasted_iota(jnp.int32, (tm, tm), 1)
    lower = jnp.where(ri > ci, 1.0, 0.0).astype(BF16)
    before = _dot(lower, onehot.astype(BF16)) + carry[...]
    r1 = jnp.sum(jnp.where(sel1, before, 0.0), axis=-1, keepdims=True)
    r2 = jnp.sum(jnp.where(sel2, before, 0.0), axis=-1, keepdims=True)
    slot_ref[...] = (jnp.where(lane == 0, i1, 0) + jnp.where(lane == 1, i2, 0)
                     + jnp.where(lane == 2, r1.astype(jnp.int32), 0)
                     + jnp.where(lane == 3, r2.astype(jnp.int32), 0))
    carry[...] += jnp.sum(onehot, axis=0, keepdims=True)
    cnt_ref[...] = carry[...].astype(jnp.int32)


def _route(y, a, r, st, wo, gain, wr, i, tm):
    rows = y.shape[0]
    l = st.layer
    row = lambda b: (b, 0)
    return pl.pallas_call(
        _route_body,
        grid=(rows // tm,),
        in_specs=[
            pl.BlockSpec((tm, D_MODEL), row),
            pl.BlockSpec((tm, MLA_WIDTH), row),
            pl.BlockSpec((tm, RET_WIDTH), row),
            _layer(wo, l), st.mod(2),
            _layer(gain, l), st.mod(3), st.mod(4),
            _layer(wr, i),
        ],
        out_specs=[pl.BlockSpec((tm, D_MODEL), row), pl.BlockSpec((tm, LANES), row),
                   pl.BlockSpec((tm, LANES), row), pl.BlockSpec((1, LANES), lambda b: (0, 0))],
        out_shape=[jax.ShapeDtypeStruct((rows, D_MODEL), F32), jax.ShapeDtypeStruct((rows, LANES), jnp.int32),
                   jax.ShapeDtypeStruct((rows, LANES), F32), jax.ShapeDtypeStruct((1, LANES), jnp.int32)],
        scratch_shapes=[pltpu.VMEM((1, LANES), F32)],
        compiler_params=_params(("arbitrary",)),
        name="route",
    )(y, a, r, wo, st.mods, gain, st.mods, st.mods, wr)


def _dispatch_body(tr, dest_ref, zt_ref, y_ref, gain_ref, sh_ref, sc_ref, xs_hbm, stage, zeros, sem, zsem):
    i = pl.program_id(0)
    tm = y_ref.shape[0]

    @pl.when(i == 0)
    def _():
        zeros[...] = jnp.zeros_like(zeros)

        def ztile(z):
            r0 = pl.multiple_of(zt_ref[z] * (tr * SUB), tr * SUB)
            return pltpu.make_async_copy(zeros, xs_hbm.at[pl.ds(r0, tr * SUB)], zsem)

        for z in range(zt_ref.shape[0]):
            @pl.when(zt_ref[z] >= 0)
            def _():
                ztile(z).start()
        for z in range(zt_ref.shape[0]):
            @pl.when(zt_ref[z] >= 0)
            def _():
                ztile(z).wait()

    h = _ffn_in(y_ref[...], gain_ref, sh_ref, sc_ref)
    for s in range(D_MODEL // LANES):
        stage[pl.ds(s, tm, stride=SUB), :] = h[:, s * LANES:(s + 1) * LANES]

    def body(t, c):
        src = stage.at[pl.ds(pl.multiple_of(t * SUB, SUB), SUB)]
        for k in range(2):
            dst = dest_ref[(i * tm + t) * 2 + k]
            pltpu.make_async_copy(src, xs_hbm.at[pl.ds(pl.multiple_of(dst * SUB, SUB), SUB)],
                                  sem).start(priority=k)
        return c

    lax.fori_loop(0, tm, body, 0, unroll=8)

    for k in range(2):
        pltpu.make_async_copy(stage, stage, sem).wait()


def _dispatch(dest_flat, ztiles, y, st, gain, tm, ntile, tr):
    rows = y.shape[0]
    grid_spec = pltpu.PrefetchScalarGridSpec(
        num_scalar_prefetch=2,
        grid=(rows // tm,),
        in_specs=[
            pl.BlockSpec((tm, D_MODEL), lambda i, dr, cr: (i, 0)),
            _layer(gain, st.layer), st.mod(3), st.mod(4),
        ],
        out_specs=pl.BlockSpec(memory_space=pl.ANY),
        scratch_shapes=[
            pltpu.VMEM((tm * SUB, LANES), F32),
            pltpu.VMEM((tr * SUB, LANES), F32),
            pltpu.SemaphoreType.DMA(()),
            pltpu.SemaphoreType.DMA(()),
        ],
    )
    return pl.pallas_call(
        functools.partial(_dispatch_body, tr),
        grid_spec=grid_spec,
        out_shape=jax.ShapeDtypeStruct((ntile * tr * SUB, LANES), F32),
        compiler_params=_params(("arbitrary",), disable_bounds_checks=True),
        name="dispatch",
    )(dest_flat, ztiles, y, gain, st.mods, st.mods)


def _experts_body(tr, te_ref, nu_ref, xs_ref, wg_ref, wu_ref, wd_ref, o_ref):
    nt = D_MODEL // LANES
    used = pl.program_id(0) < nu_ref[0]

    @pl.when(used)
    def _():
        x = jnp.concatenate([xs_ref[pl.ds(s, tr, stride=SUB), :] for s in range(nt)], axis=1).astype(BF16)
        a = (_silu(_dot(x, wg_ref[0])) * _dot(x, wu_ref[0])).astype(BF16)
        o = _dot(a, wd_ref[0])
        for s in range(nt):
            o_ref[pl.ds(s, tr, stride=SUB), :] = o[:, s * LANES:(s + 1) * LANES]

    @pl.when(jnp.logical_not(used))
    def _():
        o_ref[...] = jnp.zeros_like(o_ref)


def _experts(tile_e, nused, xs, wg, wu, wd, i, tr):
    ntile = tile_e.shape[0]
    blk = lambda j, te, nu: (j, 0)
    wsel = lambda j, te, nu: (i, te[j], 0, 0)
    grid_spec = pltpu.PrefetchScalarGridSpec(
        num_scalar_prefetch=2,
        grid=(ntile,),
        in_specs=[
            pl.BlockSpec((tr * SUB, LANES), blk),
            pl.BlockSpec((None, 1, D_MODEL, EXPERT_FF), wsel),
            pl.BlockSpec((None, 1, D_MODEL, EXPERT_FF), wsel),
            pl.BlockSpec((None, 1, EXPERT_FF, D_MODEL), wsel),
        ],
        out_specs=pl.BlockSpec((tr * SUB, LANES), blk),
    )
    return pl.pallas_call(
        functools.partial(_experts_body, tr),
        grid_spec=grid_spec,
        out_shape=jax.ShapeDtypeStruct(xs.shape, F32),
        compiler_params=_params(("arbitrary",)),
        name="experts",
    )(tile_e, nused, xs, wg, wu, wd)


def _tile_plan(cnt, slots, ntile, tr):
    ptiles = (cnt + tr - 1) // tr
    tile_end = jnp.cumsum(ptiles)
    nused = tile_end[-1]
    off = (tile_end - ptiles) * tr
    experts = jnp.arange(N_EXPERTS, dtype=jnp.int32)
    dest = slots[:, 2:4] + jnp.sum(jnp.where(slots[:, 0:2, None] == experts, off, 0), axis=-1)
    j = jnp.minimum(jnp.arange(ntile), nused - 1)
    te = jnp.sum((j[:, None] >= tile_end[None, :]).astype(jnp.int32), axis=1)
    last = jnp.where(ptiles > 0, tile_end - 1, -1)
    tail = nused + jnp.arange(ntile - slots.shape[0] * 2 // tr)
    ztiles = jnp.concatenate([last, jnp.where(tail < ntile, tail, -1)])
    i32 = lambda a: a.astype(jnp.int32)
    return i32(dest).reshape(-1), i32(te), i32(nused).reshape(1), i32(ztiles)


def _combine_body(dest_ref, y_ref, g2_ref, w_ref, y2_hbm, o_ref, stage, sem):
    i = pl.program_id(0)
    tm = y_ref.shape[0]

    def body(t, c):
        for k in range(2):
            src = dest_ref[(i * tm + t) * 2 + k]
            pltpu.make_async_copy(y2_hbm.at[pl.ds(pl.multiple_of(src * SUB, SUB), SUB)],
                                  stage.at[k, pl.ds(pl.multiple_of(t * SUB, SUB), SUB)],
                                  sem).start(priority=k)
        return c

    lax.fori_loop(0, tm, body, 0, unroll=8)
    w = w_ref[...]
    w1 = w[:, 0:1]
    w2 = w[:, 1:2]
    for k in range(2):
        pltpu.make_async_copy(stage.at[k], stage.at[k], sem).wait()
    parts = []
    for s in range(D_MODEL // LANES):
        parts.append(w1 * stage[0, pl.ds(s, tm, stride=SUB), :] + w2 * stage[1, pl.ds(s, tm, stride=SUB), :])
    o_ref[...] = y_ref[...] + g2_ref[...] * jnp.concatenate(parts, axis=1)


def _combine(dest_flat, y, st, w, y2, tm):
    rows = y.shape[0]
    grid_spec = pltpu.PrefetchScalarGridSpec(
        num_scalar_prefetch=1,
        grid=(rows // tm,),
        in_specs=[
            pl.BlockSpec((tm, D_MODEL), lambda i, dr: (i, 0)),
            st.mod(5),
            pl.BlockSpec((tm, LANES), lambda i, dr: (i, 0)),
            pl.BlockSpec(memory_space=pl.ANY),
        ],
        out_specs=pl.BlockSpec((tm, D_MODEL), lambda i, dr: (i, 0)),
        scratch_shapes=[
            pltpu.VMEM((2, tm * SUB, LANES), F32),
            pltpu.SemaphoreType.DMA(()),
        ],
    )
    return pl.pallas_call(
        _combine_body,
        grid_spec=grid_spec,
        out_shape=jax.ShapeDtypeStruct((rows, D_MODEL), F32),
        compiler_params=_params(("arbitrary",), disable_bounds_checks=True),
        name="combine",
    )(dest_flat, y, st.mods, w, y2)


def _moe(y, a, r, st, wo, gain, wr, wg, wu, wd, i, tm, tr):
    rows = y.shape[0]
    ntile = 2 * rows // tr + N_EXPERTS
    y1, slots, w, cnt = _route(y, a, r, st, wo, gain, wr, i, tm)
    dest_flat, te, nused, ztiles = _tile_plan(cnt[0, :N_EXPERTS], slots[:, :4], ntile, tr)
    xs = _dispatch(dest_flat, ztiles, y1, st, gain, tm, ntile, tr)
    y2 = _experts(te, nused, xs, wg, wu, wd, i, tr)
    return _combine(dest_flat, y1, st, w, y2, tm)


def _rope_tables(n_tok, dim):
    rows = n_tok // GRID_W
    row = jnp.repeat(jnp.arange(rows), GRID_W)
    col = jnp.tile(jnp.arange(GRID_W), rows)
    half = dim // 2
    freqs = ROPE_BASE ** (-jnp.arange(0, half, 2, dtype=F32) / half)

    def ang(p):
        a = p.astype(F32)[:, None] * freqs[None, :]
        return jnp.concatenate([a, a], axis=-1)

    angles = jnp.concatenate([ang(row), ang(col)], axis=-1)
    cos, sin = jnp.cos(angles), jnp.sin(angles)
    first = (np.arange(dim) % half) < (half // 2)
    sa = jnp.where(first, -sin, 0.0)
    sb = jnp.where(first, 0.0, sin)
    pad = LANES - dim
    cos = jnp.pad(cos, ((0, 0), (0, pad)), constant_values=1.0)
    sa = jnp.pad(sa, ((0, 0), (0, pad)))
    sb = jnp.pad(sb, ((0, 0), (0, pad)))
    return cos, sa, sb


def _pack_w_in(w_in):
    zeros = jnp.zeros(w_in.shape[:-1] + (LANES - QK_HEAD_DIM,), w_in.dtype)
    parts = []
    for hd in range(MLA_HEADS):
        b = hd * QK_HEAD_DIM
        parts += [w_in[..., b + QK_NOPE_DIM:b + QK_HEAD_DIM], w_in[..., b:b + QK_NOPE_DIM], zeros]
    o = MLA_HEADS * QK_HEAD_DIM
    parts.append(w_in[..., o:o + KV_LORA_RANK])
    o += KV_LORA_RANK
    parts += [w_in[..., o:o + QK_ROPE_DIM],
              jnp.zeros(w_in.shape[:-1] + (LANES - QK_ROPE_DIM,), w_in.dtype)]
    o += QK_ROPE_DIM
    parts += [w_in[..., o:o + RET_WIDTH], w_in[..., o + 2 * RET_WIDTH:]]
    w_kret_t = jnp.swapaxes(w_in[..., o + RET_WIDTH:o + 2 * RET_WIDTH], -1, -2)
    return jnp.concatenate(parts, axis=-1).astype(BF16), w_kret_t.astype(BF16)


def _pack_w_ukv(w_ukv):
    lead = w_ukv.shape[:-1]
    z32 = jnp.zeros(lead + (QK_ROPE_DIM,), w_ukv.dtype)
    z64 = jnp.zeros(lead + (V_HEAD_DIM,), w_ukv.dtype)
    kp, vp = [], []
    for hd in range(MLA_HEADS):
        b = hd * (QK_NOPE_DIM + V_HEAD_DIM)
        kp += [z32, w_ukv[..., b:b + QK_NOPE_DIM], z32]
        v = w_ukv[..., b + QK_NOPE_DIM:b + QK_NOPE_DIM + V_HEAD_DIM]
        vp += [v, z64] if hd % 2 == 0 else [z64, v]
    return jnp.concatenate(kp, axis=-1).astype(BF16), jnp.concatenate(vp, axis=-1).astype(BF16)


def _pack_head_gain(g):
    z = jnp.zeros(g.shape[:-1] + (LANES - QK_HEAD_DIM,), g.dtype)
    return jnp.concatenate([g[..., QK_NOPE_DIM:], g[..., :QK_NOPE_DIM], z], axis=-1)


def kernel(x_prompt, x_sample, cache_ckv, cache_krope, state_ret, c, c_ctx, attn_norm, ffn_norm, w_ada, b_ada, w_in, kv_norm, w_ukv, q_norm, k_norm, decay_logit, w_out, w_ffn_gate, w_ffn_up, w_ffn_down, w_router, w_exp_gate, w_exp_up, w_exp_down):
    nb_c, seq_c, _ = x_prompt.shape
    nb_l, seq_l, _ = x_sample.shape
    tm = ROW_BLOCK

    wp, wkt = _pack_w_in(w_in)
    wk, wv = _pack_w_ukv(w_ukv)
    wo = w_out.astype(BF16)
    wfg, wfu, wfd = w_ffn_gate.astype(BF16), w_ffn_up.astype(BF16), w_ffn_down.astype(BF16)
    weg, weu, wed = w_exp_gate.astype(BF16), w_exp_up.astype(BF16), w_exp_down.astype(BF16)
    wr = jnp.pad(w_router, ((0, 0), (0, 0), (0, LANES - N_EXPERTS)))
    wrh = wr.astype(BF16)
    wr2 = jnp.concatenate([wrh, (wr - wrh.astype(F32)).astype(BF16)], axis=-1)
    qg = _pack_head_gain(q_norm)[:, None, :]
    kg = _pack_head_gain(k_norm)[:, None, :]
    kvg = kv_norm[:, None, :]
    lg = jax.nn.log_sigmoid(decay_logit.astype(F32))
    tabs_m = _rope_tables(seq_l, QK_ROPE_DIM)
    tabs_r = _rope_tables(seq_l, RET_DK)
    tabs_rt = tuple(t.T for t in tabs_r)
    cache_kr_pad = jnp.pad(cache_krope, ((0, 0), (0, 0), (0, 0), (0, LANES - QK_ROPE_DIM)))

    cs = jnp.concatenate([c_ctx[None, :], c, jnp.zeros((SUB - 1 - nb_l, D_MODEL), F32)], axis=0)
    mods = _adaln(cs, w_ada, b_ada).reshape(DEPTH, SUB, 6, 1, D_MODEL)

    yp = x_prompt.reshape(nb_c * seq_c, D_MODEL)
    ys = x_sample.reshape(nb_l * seq_l, D_MODEL)
    ckv_list, krope_list, state_list = [], [], []
    an, fn = attn_norm[:, None, :], ffn_norm[:, None, :]
    for l in range(DEPTH):
        for ctx in (True, False):
            if ctx:
                y, st = yp, _Stream(mods, l, 0, nb_c * seq_c // tm)
            else:
                y, st = ys, _Stream(mods, l, 1, seq_l // tm)
            q, ckv, kr, qr, kret, vr, g = _inproj(
                y, st, an, wp, wkt, qg, kvg, None if ctx else tabs_m + tabs_r + tabs_rt, tm)
            if ctx:
                a = _attn_ctx(q, ckv, kr, wk, wv, kg, l, seq_c)
                r, state = _retention(lg, l, qr, kret, vr, g, None, seq_c, RET_CTX_SEQS)
                ckv_list.append(ckv.reshape(nb_c, seq_c, KV_LORA_RANK))
                krope_list.append(kr[:, :QK_ROPE_DIM].reshape(nb_c, seq_c, QK_ROPE_DIM))
                state_list.append(state)
            else:
                a = _attn_lat(q, ckv, kr, cache_ckv, cache_kr_pad, wk, wv, kg, l, tabs_m, seq_l, Q_BLOCK)
                r, = _retention(lg, l, qr, kret, vr, g, state_ret, seq_l, 1)
            i = l // 2
            if l % 2 == 0:
                y = _ffn(y, a, r, st, wo, fn, wfg, wfu, wfd, i, tm)
            else:
                y = _moe(y, a, r, st, wo, fn, wr2, weg, weu, wed, i, tm, EXPERT_TILE)
            if ctx:
                yp = y
            else:
                ys = y
    return (yp.reshape(nb_c, seq_c, D_MODEL), ys.reshape(nb_l, seq_l, D_MODEL),
            jnp.stack(ckv_list, axis=1), jnp.stack(krope_list, axis=1), jnp.stack(state_list, axis=1))
```

```python
import functools
import math

import jax
import jax.numpy as jnp
import numpy as np
from jax import lax
from jax.experimental import pallas as pl
from jax.experimental.pallas import tpu as pltpu

D_MODEL = 1024
DEPTH = 4
GRID_W = 64
MLA_HEADS = 8
QK_NOPE_DIM = 64
QK_ROPE_DIM = 32
QK_HEAD_DIM = QK_NOPE_DIM + QK_ROPE_DIM
V_HEAD_DIM = 64
KV_LORA_RANK = 256
MLA_WIDTH = MLA_HEADS * V_HEAD_DIM
RET_HEADS = 4
RET_DK = 128
RET_DV = 128
RET_CHUNK = 128
RET_WIDTH = RET_HEADS * RET_DV
D_FF = 2816
N_EXPERTS = 8
EXPERT_FF = 1408
ROPE_BASE = 10000.0
EPS = 1e-6

LANES = 128
SUB = 8
QP_WIDTH = MLA_HEADS * LANES
OFF_Q = 0
OFF_CKV = OFF_Q + QP_WIDTH
OFF_KR = OFF_CKV + KV_LORA_RANK
OFF_QR = OFF_KR + LANES
OFF_VR = OFF_QR + RET_WIDTH
OFF_G = OFF_VR + RET_WIDTH
IN_COLS_P = OFF_G + RET_WIDTH

VMEM_LIMIT = 56 * 1024 * 1024
LOG2E = math.log2(math.e)

ROW_BLOCK = 512
DMA_ROW_BLOCK = 1024
Q_BLOCK = 512
EXPERT_TILE = 256
RET_CTX_SEQS = 4

F32 = jnp.float32
BF16 = jnp.bfloat16


def _dot(a, b):
    return jnp.dot(a, b, preferred_element_type=F32)


def _dot_nt(a, b):
    return lax.dot_general(a, b, (((1,), (1,)), ((), ())), preferred_element_type=F32)


def _dot_tn(a, b):
    return lax.dot_general(a, b, (((0,), (0,)), ((), ())), preferred_element_type=F32)


def _rms(x, n):
    return x * lax.rsqrt(jnp.sum(x * x, axis=-1, keepdims=True) * (1.0 / n) + EPS)


def _silu(x):
    return x * (1.0 / (1.0 + jnp.exp(-x)))


def _rope(t, cos, sa, sb, q):
    return t * cos + pltpu.roll(t, LANES - q, 1) * sa + pltpu.roll(t, q, 1) * sb


def _rope_t(t, cos, sa, sb, q):
    n = t.shape[0]
    up = jnp.concatenate([t[q:], t[:q]], axis=0)
    down = jnp.concatenate([t[n - q:], t[:n - q]], axis=0)
    return t * cos + up * sa + down * sb


def _params(sem, **kw):
    return pltpu.CompilerParams(dimension_semantics=sem, vmem_limit_bytes=VMEM_LIMIT, **kw)


def _layer(arr, idx, **kw):
    nd = arr.ndim
    return pl.BlockSpec((None,) + arr.shape[1:], lambda *g: (idx,) + (0,) * (nd - 1), **kw)


class _Stream:
    def __init__(self, mods, layer, base, blocks_per_entry):
        self.mods, self.layer, self.base, self.bpe = mods, layer, base, blocks_per_entry

    def mod(self, which):
        l, base, bpe = self.layer, self.base, self.bpe
        return pl.BlockSpec((None, None, None, 1, D_MODEL), lambda i, *_: (l, base + i // bpe, which, 0, 0))


def _adaln_body(c_ref, w_ref, b_ref, o_ref):
    x = _silu(c_ref[...]).astype(BF16)
    o_ref[0] = _dot(x, w_ref[0].astype(BF16)) + b_ref[0]


def _adaln(cs, w_ada, b_ada):
    tn = 1536
    nb = cs.shape[0]
    return pl.pallas_call(
        _adaln_body,
        grid=(DEPTH, 6 * D_MODEL // tn),
        in_specs=[
            pl.BlockSpec((nb, D_MODEL), lambda l, j: (0, 0)),
            pl.BlockSpec((1, D_MODEL, tn), lambda l, j: (l, 0, j)),
            pl.BlockSpec((1, 1, tn), lambda l, j: (l, 0, j)),
        ],
        out_specs=pl.BlockSpec((1, nb, tn), lambda l, j: (l, 0, j)),
        out_shape=jax.ShapeDtypeStruct((DEPTH, nb, 6 * D_MODEL), F32),
        compiler_params=_params(("arbitrary", "arbitrary")),
        name="adaln",
    )(cs, w_ada, b_ada.reshape(DEPTH, 1, 6 * D_MODEL))


def _inproj_body(rope, y_ref, gain_ref, sh_ref, sc_ref, w_ref, wkt_ref, qg_ref, kvg_ref, *rest):
    if rope:
        cm_ref, sam_ref, sbm_ref, cr_ref, sar_ref, sbr_ref, crt_ref, sart_ref, sbrt_ref = rest[:9]
        rest = rest[9:]
    q_ref, ckv_ref, kr_ref, qr_ref, kt_ref, vr_ref, g_ref = rest
    tm = y_ref.shape[0]

    half = tm // 2
    rows = [pl.ds(0, half), pl.ds(half, half)]

    hbs = []
    for rs in rows:
        h = _rms(y_ref[rs, :], D_MODEL) * gain_ref[...]
        hbs.append((h * (1.0 + sc_ref[...]) + sh_ref[...]).astype(BF16))

    prods = []
    for rs, hb in zip(rows, hbs):
        pq = _dot(hb, w_ref[:, OFF_Q:OFF_Q + QP_WIDTH])
        pckv = _dot(hb, w_ref[:, OFF_CKV:OFF_CKV + KV_LORA_RANK])
        kr_ref[rs, :] = _dot(hb, w_ref[:, OFF_KR:OFF_KR + LANES])
        pqr = _dot(hb, w_ref[:, OFF_QR:OFF_QR + RET_WIDTH])
        kt = _dot_nt(wkt_ref[...], hb)
        vr_ref[rs, :] = _dot(hb, w_ref[:, OFF_VR:OFF_VR + RET_WIDTH]).astype(BF16)
        g_ref[rs, :] = _dot(hb, w_ref[:, OFF_G:OFF_G + RET_WIDTH]).astype(BF16)
        prods.append((pq, pckv, pqr, kt))

    qg = qg_ref[...]
    for part, (rs, (pq, pckv, pqr, kt)) in enumerate(zip(rows, prods)):
        for hd in range(MLA_HEADS):
            t = _rms(pq[:, hd * LANES:(hd + 1) * LANES], QK_HEAD_DIM) * qg
            if rope:
                t = _rope(t, cm_ref[rs, :], sam_ref[rs, :], sbm_ref[rs, :], QK_ROPE_DIM // 4)
            q_ref[rs, hd * LANES:(hd + 1) * LANES] = t.astype(BF16)

        ckv_ref[rs, :] = _rms(pckv, KV_LORA_RANK) * kvg_ref[...]

        for hd in range(RET_HEADS):
            c0 = hd * LANES
            t = pqr[:, c0:c0 + LANES]
            if rope:
                t = _rope(t, cr_ref[rs, :], sar_ref[rs, :], sbr_ref[rs, :], RET_DK // 4)
            qr_ref[rs, c0:c0 + LANES] = t.astype(BF16)
        for hd in range(RET_HEADS):
            t = kt[hd * RET_DK:(hd + 1) * RET_DK, :]
            if rope:
                t = _rope_t(t, crt_ref[:, rs], sart_ref[:, rs], sbrt_ref[:, rs], RET_DK // 4)
            t = (t * (RET_DK ** -0.5)).astype(BF16)
            for j in range(half // RET_CHUNK):
                kt_ref[part * (half // RET_CHUNK) + j, hd * RET_DK:(hd + 1) * RET_DK, :] = (
                    t[:, j * RET_CHUNK:(j + 1) * RET_CHUNK])


def _inproj(y, st, gain, wp, wkt, qg, kvg, tables, tm):
    rows = y.shape[0]
    rope = tables is not None
    l, bpb = st.layer, st.bpe
    cpb = tm // RET_CHUNK
    in_specs = [
        pl.BlockSpec((tm, D_MODEL), lambda i: (i, 0)),
        _layer(gain, l), st.mod(0), st.mod(1), _layer(wp, l), _layer(wkt, l), _layer(qg, l), _layer(kvg, l),
    ]
    args = [y, gain, st.mods, st.mods, wp, wkt, qg, kvg]
    if rope:
        in_specs += [pl.BlockSpec((tm, LANES), lambda i: (i % bpb, 0))] * 6
        in_specs += [pl.BlockSpec((RET_DK, tm), lambda i: (0, i % bpb))] * 3
        args += list(tables)
    blocks = [(tm, QP_WIDTH), (tm, KV_LORA_RANK), (tm, LANES), (tm, RET_WIDTH),
              (cpb, RET_WIDTH, RET_CHUNK), (tm, RET_WIDTH), (tm, RET_WIDTH)]
    dtypes = (BF16, F32, F32, BF16, BF16, BF16, BF16)
    out_specs, out_shape = [], []
    for blk, dt in zip(blocks, dtypes):
        nd = len(blk)
        out_specs.append(pl.BlockSpec(blk, lambda i, nd=nd: (i,) + (0,) * (nd - 1)))
        out_shape.append(jax.ShapeDtypeStruct((rows // tm * blk[0],) + blk[1:], dt))
    return pl.pallas_call(
        functools.partial(_inproj_body, rope),
        grid=(rows // tm,),
        in_specs=in_specs,
        out_specs=out_specs,
        out_shape=out_shape,
        compiler_params=_params(("arbitrary",)),
        name="inproj_rope" if rope else "inproj",
    )(*args)


def _kv_up(ckv_f32, wk_ref, wv_ref):
    cb = ckv_f32.astype(BF16)
    return _dot(cb, wk_ref[...]), _dot(cb, wv_ref[...])


def _make_kv(up, kr_pad, kg, rope_tabs):
    kall, vall = up
    ks, vs = [], []
    for hd in range(MLA_HEADS):
        k = kall[:, hd * LANES:(hd + 1) * LANES] + kr_pad
        k = _rms(k, QK_HEAD_DIM) * kg
        if rope_tabs is not None:
            k = _rope(k, *rope_tabs, QK_ROPE_DIM // 4)
        ks.append(k.astype(BF16))
        vs.append(vall[:, hd * LANES:(hd + 1) * LANES].astype(BF16))
    return ks, vs


def _softmax_pv(s, v):
    m = jnp.max(s, axis=-1, keepdims=True)
    p = jnp.exp2((s - m) * (QK_HEAD_DIM ** -0.5 * LOG2E))
    l = jnp.sum(p, axis=-1, keepdims=True)
    return _dot(p.astype(BF16), v) * (1.0 / l)


def _attend(q_ref, get_k, get_v, o_ref):
    def scores(hd):
        return _dot_nt(q_ref[:, hd * LANES:(hd + 1) * LANES], get_k(hd))

    s_next = scores(0)
    acc = None
    for hd in range(MLA_HEADS):
        s = s_next
        if hd + 1 < MLA_HEADS:
            s_next = scores(hd + 1)
        o = _softmax_pv(s, get_v(hd))
        acc = o if hd % 2 == 0 else acc + o
        if hd % 2 == 1:
            o_ref[:, (hd // 2) * LANES:(hd // 2 + 1) * LANES] = acc.astype(BF16)


def _attn_ctx_body(q_ref, ckv_ref, kr_ref, wk_ref, wv_ref, kg_ref, o_ref):
    ks, vs = _make_kv(_kv_up(ckv_ref[...], wk_ref, wv_ref), kr_ref[...], kg_ref[...], None)
    _attend(q_ref, lambda hd: ks[hd], lambda hd: vs[hd], o_ref)


def _attn_ctx(q, ckv, kr, wk, wv, kg, l, seq):
    rows = q.shape[0]
    row = lambda b: (b, 0)
    return pl.pallas_call(
        _attn_ctx_body,
        grid=(rows // seq,),
        in_specs=[
            pl.BlockSpec((seq, QP_WIDTH), row),
            pl.BlockSpec((seq, KV_LORA_RANK), row),
            pl.BlockSpec((seq, LANES), row),
            _layer(wk, l), _layer(wv, l), _layer(kg, l),
        ],
        out_specs=pl.BlockSpec((seq, MLA_WIDTH), row),
        out_shape=jax.ShapeDtypeStruct((rows, MLA_WIDTH), BF16),
        compiler_params=_params(("arbitrary",)),
        name="attn_ctx",
    )(q, ckv, kr, wk, wv, kg)


def _attn_lat_body(past, seq, tq, q_ref, ckv_ref, kr_ref, cckv_ref, ckr_ref, wk_ref, wv_ref, kg_ref,
                   cm_ref, sam_ref, sbm_ref, o_ref, k_s, v_s):
    kc = 256

    @pl.when(pl.program_id(1) == 0)
    def _():
        kg = kg_ref[...]
        for c in range(past // kc):
            r = slice(c * kc, (c + 1) * kc)
            ks, vs = _make_kv(_kv_up(cckv_ref[r, :], wk_ref, wv_ref), ckr_ref[r, :], kg, None)
            for hd in range(MLA_HEADS):
                k_s[hd, r, :] = ks[hd]
                v_s[hd, r, :] = vs[hd]

        def chunk(c, carry):
            r0 = pl.multiple_of(c * kc, kc)
            rs = pl.ds(r0, kc)
            tabs = (cm_ref[rs, :], sam_ref[rs, :], sbm_ref[rs, :])
            ks, vs = _make_kv(_kv_up(ckv_ref[rs, :], wk_ref, wv_ref), kr_ref[rs, :], kg, tabs)
            ro = pl.ds(past + r0, kc)
            for hd in range(MLA_HEADS):
                k_s[hd, ro, :] = ks[hd]
                v_s[hd, ro, :] = vs[hd]
            return carry

        lax.fori_loop(0, seq // kc, chunk, 0)

    _attend(q_ref, lambda hd: k_s[hd], lambda hd: v_s[hd], o_ref)


def _attn_lat(q, ckv, kr, cache_ckv, cache_kr, wk, wv, kg, l, tabs_m, seq, tq):
    rows = q.shape[0]
    nb = rows // seq
    nq = seq // tq
    past = cache_ckv.shape[2]
    const = lambda b, i: (0, 0)
    return pl.pallas_call(
        functools.partial(_attn_lat_body, past, seq, tq),
        grid=(nb, nq),
        in_specs=[
            pl.BlockSpec((tq, QP_WIDTH), lambda b, i: (b * nq + i, 0)),
            pl.BlockSpec((seq, KV_LORA_RANK), lambda b, i: (b, 0)),
            pl.BlockSpec((seq, LANES), lambda b, i: (b, 0)),
            pl.BlockSpec((None, None, past, KV_LORA_RANK), lambda b, i: (b, l, 0, 0)),
            pl.BlockSpec((None, None, past, LANES), lambda b, i: (b, l, 0, 0)),
            _layer(wk, l), _layer(wv, l), _layer(kg, l),
            pl.BlockSpec((seq, LANES), const),
            pl.BlockSpec((seq, LANES), const),
            pl.BlockSpec((seq, LANES), const),
        ],
        out_specs=pl.BlockSpec((tq, MLA_WIDTH), lambda b, i: (b * nq + i, 0)),
        out_shape=jax.ShapeDtypeStruct((rows, MLA_WIDTH), BF16),
        scratch_shapes=[
            pltpu.VMEM((MLA_HEADS, past + seq, LANES), BF16),
            pltpu.VMEM((MLA_HEADS, past + seq, LANES), BF16),
        ],
        compiler_params=_params(("arbitrary", "arbitrary")),
        name="attn_lat",
    )(q, ckv, kr, cache_ckv, cache_kr, wk, wv, kg, *tabs_m)


def _ret_body(seq, nseq, has_state, layer, lg_ref, q_ref, kt_ref, v_ref, g_ref, *rest):
    if has_state:
        s0_ref, o_ref, acc_s, u_s, st_s, dec_s = rest
    else:
        o_ref, stout_ref, acc_s, u_s, st_s, dec_s = rest
    C = RET_CHUNK
    nc = seq // C
    ii = lax.broadcasted_iota(jnp.int32, (C, C), 0)
    jj = lax.broadcasted_iota(jnp.int32, (C, C), 1)
    diff = (ii - jj).astype(F32)
    pos = ii.astype(F32)
    post = jj.astype(F32)
    cfull = jnp.full((1, LANES), float(C), F32)
    cdf, cdb = [], []
    for hd in range(RET_HEADS):
        lgf = lg_ref[layer, 0, hd]
        lgb = lg_ref[layer, 1, hd]
        dec_s[hd, 0] = (jnp.where(diff >= 0, jnp.exp(lgf * jnp.maximum(diff, 0.0)), 0.0)
                        + jnp.where(diff <= 0, jnp.exp(lgb * jnp.maximum(-diff, 0.0)), 0.0))
        dec_s[hd, 1] = jnp.exp(lgf * (pos + 1.0))
        dec_s[hd, 2] = jnp.exp(lgb * (C - pos))
        dec_s[hd, 3] = jnp.exp(lgf * (C - 1.0 - post))
        dec_s[hd, 4] = jnp.exp(lgb * post)
        cdf.append(jnp.exp(lgf * cfull))
        cdb.append(jnp.exp(lgb * cfull))

    per_trip = 2
    items = [(sub, hd) for sub in range(per_trip) for hd in range(RET_HEADS)]
    cols = [slice(hd * LANES, (hd + 1) * LANES) for hd in range(RET_HEADS)]

    def p1(trip, carry):
        cs = [trip * per_trip + sub for sub in range(per_trip)]
        rs = [pl.ds(pl.multiple_of(c * C, C), C) for c in cs]
        kts = {(s, h): kt_ref[cs[s], h * RET_DK:(h + 1) * RET_DK, :] for s, h in items}
        vcs = {(s, h): v_ref[rs[s], cols[h]] for s, h in items}
        scores = {(s, h): _dot(q_ref[rs[s], cols[h]], kts[s, h]) for s, h in items}
        for s, h in items:
            kf = kts[s, h].astype(F32)
            kd = jnp.concatenate([(kf * dec_s[h, 3]).astype(BF16), (kf * dec_s[h, 4]).astype(BF16)], axis=0)
            u_s[cs[s], h] = _dot(kd, vcs[s, h])
        for s, h in items:
            acc_s[rs[s], cols[h]] = _dot((scores[s, h] * dec_s[h, 0]).astype(BF16), vcs[s, h])
        return carry

    lax.fori_loop(0, nseq * nc // per_trip, p1, 0)

    for sq in range(nseq):
        for hd in range(RET_HEADS):
            if has_state:
                init = (s0_ref[sq, 0, hd], s0_ref[sq, 1, hd])
            else:
                init = (jnp.zeros((RET_DK, RET_DV), F32), jnp.zeros((RET_DK, RET_DV), F32))

            def p2(t, carry, sq=sq, hd=hd):
                sf, sb = carry
                cf = sq * nc + t
                cb = sq * nc + (nc - 1 - t)
                st_s[cf, hd, :, 0:RET_DV] = sf.astype(BF16)
                st_s[cb, hd, :, RET_DV:2 * RET_DV] = sb.astype(BF16)
                sf = sf * cdf[hd] + u_s[cf, hd, 0:RET_DK, :]
                sb = sb * cdb[hd] + u_s[cb, hd, RET_DK:2 * RET_DK, :]
                return sf, sb

            sf, sb = lax.fori_loop(0, nc, p2, init)
            if not has_state:
                stout_ref[sq, 0, hd] = sf
                stout_ref[sq, 1, hd] = sb

    def p3(trip, carry):
        cs = [trip * per_trip + sub for sub in range(per_trip)]
        rs = [pl.ds(pl.multiple_of(c * C, C), C) for c in cs]
        crs = {(s, h): _dot(q_ref[rs[s], cols[h]], st_s[cs[s], h]) for s, h in items}
        for s, h in items:
            cr = crs[s, h]
            r = acc_s[rs[s], cols[h]] + cr[:, 0:RET_DV] * dec_s[h, 1] + cr[:, RET_DV:2 * RET_DV] * dec_s[h, 2]
            gh = g_ref[rs[s], cols[h]].astype(F32)
            o_ref[rs[s], cols[h]] = (_rms(r, RET_DV) * _silu(gh)).astype(BF16)
        return carry

    lax.fori_loop(0, nseq * nc // per_trip, p3, 0)


def _retention(lg, l, qr, kt, vr, g, s0, seq, nseq):
    rows = qr.shape[0]
    nb = rows // seq
    has_state = s0 is not None
    nct = nseq * seq // RET_CHUNK
    row = lambda b: (b, 0)
    if has_state:
        st_spec = pl.BlockSpec((nseq, None, 2, RET_HEADS, RET_DK, RET_DV), lambda b: (b, l, 0, 0, 0, 0))
    else:
        st_spec = pl.BlockSpec((nseq, 2, RET_HEADS, RET_DK, RET_DV), lambda b: (b, 0, 0, 0, 0))
    tok = pl.BlockSpec((nseq * seq, RET_WIDTH), row)
    in_specs = [pl.BlockSpec(memory_space=pltpu.SMEM), tok,
                pl.BlockSpec((nct, RET_WIDTH, RET_CHUNK), lambda b: (b, 0, 0)), tok, tok]
    args = [lg, qr, kt, vr, g]
    out_specs = [pl.BlockSpec((nseq * seq, RET_WIDTH), row)]
    out_shape = [jax.ShapeDtypeStruct((rows, RET_WIDTH), BF16)]
    if has_state:
        in_specs.append(st_spec)
        args.append(s0)
    else:
        out_specs.append(st_spec)
        out_shape.append(jax.ShapeDtypeStruct((nb, 2, RET_HEADS, RET_DK, RET_DV), F32))
    return pl.pallas_call(
        functools.partial(_ret_body, seq, nseq, has_state, l),
        grid=(nb // nseq,),
        in_specs=in_specs,
        out_specs=out_specs,
        out_shape=out_shape,
        scratch_shapes=[
            pltpu.VMEM((nseq * seq, RET_WIDTH), F32),
            pltpu.VMEM((nct, RET_HEADS, 2 * RET_DK, RET_DV), F32),
            pltpu.VMEM((nct, RET_HEADS, RET_DK, 2 * RET_DV), BF16),
            pltpu.VMEM((RET_HEADS, 5, RET_CHUNK, RET_CHUNK), F32),
        ],
        compiler_params=_params(("arbitrary",)),
        name="retention_lat" if has_state else "retention_ctx",
    )(*args)


def _mixer_residual(y_ref, a_ref, r_ref, wo_ref, g1_ref):
    out = _dot(a_ref[...], wo_ref[:MLA_WIDTH, :]) + _dot(r_ref[...], wo_ref[MLA_WIDTH:, :])
    return y_ref[...] + g1_ref[...] * out


def _ffn_in(y, gain_ref, sh_ref, sc_ref):
    h = _rms(y, D_MODEL) * gain_ref[...]
    return h * (1.0 + sc_ref[...]) + sh_ref[...]


def _ffn_body(y_ref, a_ref, r_ref, wo_ref, g1_ref, gain_ref, sh_ref, sc_ref, g2_ref, wg_ref, wu_ref, wd_ref, o_ref):
    tm = y_ref.shape[0]
    halves = [pl.ds(0, tm // 2), pl.ds(tm // 2, tm // 2)]
    mix = [_dot(a_ref[h, :], wo_ref[:MLA_WIDTH, :]) + _dot(r_ref[h, :], wo_ref[MLA_WIDTH:, :]) for h in halves]
    ys = [y_ref[h, :] + g1_ref[...] * m for h, m in zip(halves, mix)]
    hbs = [_ffn_in(y, gain_ref, sh_ref, sc_ref).astype(BF16) for y in ys]
    gus = [(_dot(hb, wg_ref[...]), _dot(hb, wu_ref[...])) for hb in hbs]
    acts = [(_silu(g) * u).astype(BF16) for g, u in gus]
    for h, y, act in zip(halves, ys, acts):
        o_ref[h, :] = y + g2_ref[...] * _dot(act, wd_ref[...])


def _ffn(y, a, r, st, wo, gain, wg, wu, wd, i, tm):
    rows = y.shape[0]
    l = st.layer
    row = lambda b: (b, 0)
    once = pl.Buffered(1)
    return pl.pallas_call(
        _ffn_body,
        grid=(rows // tm,),
        in_specs=[
            pl.BlockSpec((tm, D_MODEL), row),
            pl.BlockSpec((tm, MLA_WIDTH), row),
            pl.BlockSpec((tm, RET_WIDTH), row),
            _layer(wo, l, pipeline_mode=once), st.mod(2),
            _layer(gain, l), st.mod(3), st.mod(4), st.mod(5),
            _layer(wg, i, pipeline_mode=once), _layer(wu, i, pipeline_mode=once), _layer(wd, i, pipeline_mode=once),
        ],
        out_specs=pl.BlockSpec((tm, D_MODEL), row),
        out_shape=jax.ShapeDtypeStruct((rows, D_MODEL), F32),
        compiler_params=_params(("arbitrary",)),
        name="ffn",
    )(y, a, r, wo, st.mods, gain, st.mods, st.mods, st.mods, wg, wu, wd)


def _split_bf16(x):
    hi = x.astype(BF16)
    return hi, (x - hi.astype(F32)).astype(BF16)


def _route_body(y_ref, a_ref, r_ref, wo_ref, g1_ref, gain_ref, sh_ref, sc_ref, wr_ref,
                y1_ref, slot_ref, w_ref, cnt_ref, carry):
    tm = y_ref.shape[0]

    @pl.when(pl.program_id(0) == 0)
    def _():
        carry[...] = jnp.zeros_like(carry)

    y = _mixer_residual(y_ref, a_ref, r_ref, wo_ref, g1_ref)
    y1_ref[...] = y
    h = _ffn_in(y, gain_ref, sh_ref, sc_ref)
    hi, lo = _split_bf16(h)
    hh_hl = _dot(hi, wr_ref[...])
    logits = hh_hl[:, :LANES] + (_dot(lo, wr_ref[:, :LANES]) + hh_hl[:, LANES:])
    lane = lax.broadcasted_iota(jnp.int32, logits.shape, 1)
    ninf = jnp.float32(-jnp.inf)
    lg = jnp.where(lane < N_EXPERTS, logits, ninf)
    m1 = jnp.max(lg, axis=-1, keepdims=True)
    i1 = jnp.min(jnp.where(lg == m1, lane, LANES), axis=-1, keepdims=True)
    lg2 = jnp.where(lane == i1, ninf, lg)
    m2 = jnp.max(lg2, axis=-1, keepdims=True)
    i2 = jnp.min(jnp.where(lg2 == m2, lane, LANES), axis=-1, keepdims=True)
    ex = jnp.exp(m2 - m1)
    den = 1.0 / (1.0 + ex)
    w_ref[...] = jnp.where(lane == 0, den, 0.0) + jnp.where(lane == 1, ex * den, 0.0)

    sel1 = lane == i1
    sel2 = lane == i2
    onehot = jnp.where(sel1 | sel2, 1.0, 0.0)
    ri = lax.broadcasted_iota(jnp.int32, (tm, tm), 0)
    ci = lax.broadcasted_iota(jnp.int32, (tm, tm), 1)
    lower = jnp.where(ri > ci, 1.0, 0.0).astype(BF16)
    before = _dot(lower, onehot.astype(BF16)) + carry[...]
    r1 = jnp.sum(jnp.where(sel1, before, 0.0), axis=-1, keepdims=True)
    r2 = jnp.sum(jnp.where(sel2, before, 0.0), axis=-1, keepdims=True)
    slot_ref[...] = (jnp.where(lane == 0, i1, 0) + jnp.where(lane == 1, i2, 0)
                     + jnp.where(lane == 2, r1.astype(jnp.int32), 0)
                     + jnp.where(lane == 3, r2.astype(jnp.int32), 0))
    carry[...] += jnp.sum(onehot, axis=0, keepdims=True)
    cnt_ref[...] = carry[...].astype(jnp.int32)


def _route(y, a, r, st, wo, gain, wr, i, tm):
    rows = y.shape[0]
    l = st.layer
    row = lambda b: (b, 0)
    return pl.pallas_call(
        _route_body,
        grid=(rows // tm,),
        in_specs=[
            pl.BlockSpec((tm, D_MODEL), row),
            pl.BlockSpec((tm, MLA_WIDTH), row),
            pl.BlockSpec((tm, RET_WIDTH), row),
            _layer(wo, l), st.mod(2),
            _layer(gain, l), st.mod(3), st.mod(4),
            _layer(wr, i),
        ],
        out_specs=[pl.BlockSpec((tm, D_MODEL), row), pl.BlockSpec((tm, LANES), row),
                   pl.BlockSpec((tm, LANES), row), pl.BlockSpec((1, LANES), lambda b: (0, 0))],
        out_shape=[jax.ShapeDtypeStruct((rows, D_MODEL), F32), jax.ShapeDtypeStruct((rows, LANES), jnp.int32),
                   jax.ShapeDtypeStruct((rows, LANES), F32), jax.ShapeDtypeStruct((1, LANES), jnp.int32)],
        scratch_shapes=[pltpu.VMEM((1, LANES), F32)],
        compiler_params=_params(("arbitrary",)),
        name="route",
    )(y, a, r, wo, st.mods, gain, st.mods, st.mods, wr)


def _dispatch_body(tr, dest_ref, zt_ref, y_ref, gain_ref, sh_ref, sc_ref, xs_hbm, stage, zeros, sem, zsem):
    i = pl.program_id(0)
    tm = y_ref.shape[0]

    @pl.when(i == 0)
    def _():
        zeros[...] = jnp.zeros_like(zeros)

        def ztile(z):
            r0 = pl.multiple_of(zt_ref[z] * (tr * SUB), tr * SUB)
            return pltpu.make_async_copy(zeros, xs_hbm.at[pl.ds(r0, tr * SUB)], zsem)

        for z in range(zt_ref.shape[0]):
            @pl.when(zt_ref[z] >= 0)
            def _():
                ztile(z).start()
        for z in range(zt_ref.shape[0]):
            @pl.when(zt_ref[z] >= 0)
            def _():
                ztile(z).wait()

    h = _ffn_in(y_ref[...], gain_ref, sh_ref, sc_ref)
    for s in range(D_MODEL // LANES):
        stage[pl.ds(s, tm, stride=SUB), :] = h[:, s * LANES:(s + 1) * LANES]

    def body(t, c):
        src = stage.at[pl.ds(pl.multiple_of(t * SUB, SUB), SUB)]
        for k in range(2):
            dst = dest_ref[(i * tm + t) * 2 + k]
            pltpu.make_async_copy(src, xs_hbm.at[pl.ds(pl.multiple_of(dst * SUB, SUB), SUB)],
                                  sem).start(priority=k)
        return c

    lax.fori_loop(0, tm, body, 0, unroll=8)

    for k in range(2):
        pltpu.make_async_copy(stage, stage, sem).wait()


def _dispatch(dest_flat, ztiles, y, st, gain, tm, ntile, tr):
    rows = y.shape[0]
    grid_spec = pltpu.PrefetchScalarGridSpec(
        num_scalar_prefetch=2,
        grid=(rows // tm,),
        in_specs=[
            pl.BlockSpec((tm, D_MODEL), lambda i, dr, cr: (i, 0)),
            _layer(gain, st.layer), st.mod(3), st.mod(4),
        ],
        out_specs=pl.BlockSpec(memory_space=pl.ANY),
        scratch_shapes=[
            pltpu.VMEM((tm * SUB, LANES), F32),
            pltpu.VMEM((tr * SUB, LANES), F32),
            pltpu.SemaphoreType.DMA(()),
            pltpu.SemaphoreType.DMA(()),
        ],
    )
    return pl.pallas_call(
        functools.partial(_dispatch_body, tr),
        grid_spec=grid_spec,
        out_shape=jax.ShapeDtypeStruct((ntile * tr * SUB, LANES), F32),
        compiler_params=_params(("arbitrary",), disable_bounds_checks=True),
        name="dispatch",
    )(dest_flat, ztiles, y, gain, st.mods, st.mods)


def _experts_body(tr, te_ref, nu_ref, xs_ref, wgu_ref, wd_ref, o_ref):
    nt = D_MODEL // LANES
    used = pl.program_id(0) < nu_ref[0]

    @pl.when(used)
    def _():
        x = jnp.concatenate([xs_ref[pl.ds(s, tr, stride=SUB), :] for s in range(nt)], axis=1).astype(BF16)
        gu = _dot(x, wgu_ref[0])
        a = (_silu(gu[:, :EXPERT_FF]) * gu[:, EXPERT_FF:]).astype(BF16)
        o = _dot(a, wd_ref[0])
        for s in range(nt):
            o_ref[pl.ds(s, tr, stride=SUB), :] = o[:, s * LANES:(s + 1) * LANES]

    @pl.when(jnp.logical_not(used))
    def _():
        o_ref[...] = jnp.zeros_like(o_ref)


def _experts(tile_e, nused, xs, wgu, wd, i, tr):
    ntile = tile_e.shape[0]
    blk = lambda j, te, nu: (j, 0)
    wsel = lambda j, te, nu: (i, te[j], 0, 0)
    grid_spec = pltpu.PrefetchScalarGridSpec(
        num_scalar_prefetch=2,
        grid=(ntile,),
        in_specs=[
            pl.BlockSpec((tr * SUB, LANES), blk),
            pl.BlockSpec((None, 1, D_MODEL, 2 * EXPERT_FF), wsel),
            pl.BlockSpec((None, 1, EXPERT_FF, D_MODEL), wsel),
        ],
        out_specs=pl.BlockSpec((tr * SUB, LANES), blk),
    )
    return pl.pallas_call(
        functools.partial(_experts_body, tr),
        grid_spec=grid_spec,
        out_shape=jax.ShapeDtypeStruct(xs.shape, F32),
        compiler_params=_params(("arbitrary",)),
        name="experts",
    )(tile_e, nused, xs, wgu, wd)


def _tile_plan(cnt, slots, ntile, tr):
    ptiles = (cnt + tr - 1) // tr
    tile_end = jnp.cumsum(ptiles)
    nused = tile_end[-1]
    off = (tile_end - ptiles) * tr
    experts = jnp.arange(N_EXPERTS, dtype=jnp.int32)
    dest = slots[:, 2:4] + jnp.sum(jnp.where(slots[:, 0:2, None] == experts, off, 0), axis=-1)
    j = jnp.minimum(jnp.arange(ntile), nused - 1)
    te = jnp.sum((j[:, None] >= tile_end[None, :]).astype(jnp.int32), axis=1)
    last = jnp.where(ptiles > 0, tile_end - 1, -1)
    tail = nused + jnp.arange(ntile - slots.shape[0] * 2 // tr)
    ztiles = jnp.concatenate([last, jnp.where(tail < ntile, tail, -1)])
    i32 = lambda a: a.astype(jnp.int32)
    return i32(dest).reshape(-1), i32(te), i32(nused).reshape(1), i32(ztiles)


def _combine_body(dest_ref, y_ref, g2_ref, w_ref, y2_hbm, o_ref, stage, sem):
    i = pl.program_id(0)
    tm = y_ref.shape[0]

    def body(t, c):
        for k in range(2):
            src = dest_ref[(i * tm + t) * 2 + k]
            pltpu.make_async_copy(y2_hbm.at[pl.ds(pl.multiple_of(src * SUB, SUB), SUB)],
                                  stage.at[k, pl.ds(pl.multiple_of(t * SUB, SUB), SUB)],
                                  sem).start(priority=k)
        return c

    lax.fori_loop(0, tm, body, 0, unroll=8)
    w = w_ref[...]
    w1 = w[:, 0:1]
    w2 = w[:, 1:2]
    for k in range(2):
        pltpu.make_async_copy(stage.at[k], stage.at[k], sem).wait()
    parts = []
    for s in range(D_MODEL // LANES):
        parts.append(w1 * stage[0, pl.ds(s, tm, stride=SUB), :] + w2 * stage[1, pl.ds(s, tm, stride=SUB), :])
    o_ref[...] = y_ref[...] + g2_ref[...] * jnp.concatenate(parts, axis=1)


def _combine(dest_flat, y, st, w, y2, tm):
    rows = y.shape[0]
    grid_spec = pltpu.PrefetchScalarGridSpec(
        num_scalar_prefetch=1,
        grid=(rows // tm,),
        in_specs=[
            pl.BlockSpec((tm, D_MODEL), lambda i, dr: (i, 0)),
            st.mod(5),
            pl.BlockSpec((tm, LANES), lambda i, dr: (i, 0)),
            pl.BlockSpec(memory_space=pl.ANY),
        ],
        out_specs=pl.BlockSpec((tm, D_MODEL), lambda i, dr: (i, 0)),
        scratch_shapes=[
            pltpu.VMEM((2, tm * SUB, LANES), F32),
            pltpu.SemaphoreType.DMA(()),
        ],
    )
    return pl.pallas_call(
        _combine_body,
        grid_spec=grid_spec,
        out_shape=jax.ShapeDtypeStruct((rows, D_MODEL), F32),
        compiler_params=_params(("arbitrary",), disable_bounds_checks=True),
        name="combine",
    )(dest_flat, y, st.mods, w, y2)


def _moe(y, a, r, st, wo, gain, wr, wgu, wd, i, tm, tr):
    rows = y.shape[0]
    ntile = 2 * rows // tr + N_EXPERTS
    y1, slots, w, cnt = _route(y, a, r, st, wo, gain, wr, i, tm)
    dest_flat, te, nused, ztiles = _tile_plan(cnt[0, :N_EXPERTS], slots[:, :4], ntile, tr)
    st_dma = _Stream(st.mods, st.layer, st.base, st.bpe * tm // DMA_ROW_BLOCK)
    xs = _dispatch(dest_flat, ztiles, y1, st_dma, gain, DMA_ROW_BLOCK, ntile, tr)
    y2 = _experts(te, nused, xs, wgu, wd, i, tr)
    return _combine(dest_flat, y1, st_dma, w, y2, DMA_ROW_BLOCK)


def _rope_tables(n_tok, dim):
    rows = n_tok // GRID_W
    row = jnp.repeat(jnp.arange(rows), GRID_W)
    col = jnp.tile(jnp.arange(GRID_W), rows)
    half = dim // 2
    freqs = ROPE_BASE ** (-jnp.arange(0, half, 2, dtype=F32) / half)

    def ang(p):
        a = p.astype(F32)[:, None] * freqs[None, :]
        return jnp.concatenate([a, a], axis=-1)

    angles = jnp.concatenate([ang(row), ang(col)], axis=-1)
    cos, sin = jnp.cos(angles), jnp.sin(angles)
    first = (np.arange(dim) % half) < (half // 2)
    sa = jnp.where(first, -sin, 0.0)
    sb = jnp.where(first, 0.0, sin)
    pad = LANES - dim
    cos = jnp.pad(cos, ((0, 0), (0, pad)), constant_values=1.0)
    sa = jnp.pad(sa, ((0, 0), (0, pad)))
    sb = jnp.pad(sb, ((0, 0), (0, pad)))
    return cos, sa, sb


def _pack_w_in(w_in):
    zeros = jnp.zeros(w_in.shape[:-1] + (LANES - QK_HEAD_DIM,), w_in.dtype)
    parts = []
    for hd in range(MLA_HEADS):
        b = hd * QK_HEAD_DIM
        parts += [w_in[..., b + QK_NOPE_DIM:b + QK_HEAD_DIM], w_in[..., b:b + QK_NOPE_DIM], zeros]
    o = MLA_HEADS * QK_HEAD_DIM
    parts.append(w_in[..., o:o + KV_LORA_RANK])
    o += KV_LORA_RANK
    parts += [w_in[..., o:o + QK_ROPE_DIM],
              jnp.zeros(w_in.shape[:-1] + (LANES - QK_ROPE_DIM,), w_in.dtype)]
    o += QK_ROPE_DIM
    parts += [w_in[..., o:o + RET_WIDTH], w_in[..., o + 2 * RET_WIDTH:]]
    w_kret_t = jnp.swapaxes(w_in[..., o + RET_WIDTH:o + 2 * RET_WIDTH], -1, -2)
    return jnp.concatenate(parts, axis=-1).astype(BF16), w_kret_t.astype(BF16)


def _pack_w_ukv(w_ukv):
    lead = w_ukv.shape[:-1]
    z32 = jnp.zeros(lead + (QK_ROPE_DIM,), w_ukv.dtype)
    z64 = jnp.zeros(lead + (V_HEAD_DIM,), w_ukv.dtype)
    kp, vp = [], []
    for hd in range(MLA_HEADS):
        b = hd * (QK_NOPE_DIM + V_HEAD_DIM)
        kp += [z32, w_ukv[..., b:b + QK_NOPE_DIM], z32]
        v = w_ukv[..., b + QK_NOPE_DIM:b + QK_NOPE_DIM + V_HEAD_DIM]
        vp += [v, z64] if hd % 2 == 0 else [z64, v]
    return jnp.concatenate(kp, axis=-1).astype(BF16), jnp.concatenate(vp, axis=-1).astype(BF16)


def _pack_head_gain(g):
    z = jnp.zeros(g.shape[:-1] + (LANES - QK_HEAD_DIM,), g.dtype)
    return jnp.concatenate([g[..., QK_NOPE_DIM:], g[..., :QK_NOPE_DIM], z], axis=-1)


def kernel(x_prompt, x_sample, cache_ckv, cache_krope, state_ret, c, c_ctx, attn_norm, ffn_norm, w_ada, b_ada, w_in, kv_norm, w_ukv, q_norm, k_norm, decay_logit, w_out, w_ffn_gate, w_ffn_up, w_ffn_down, w_router, w_exp_gate, w_exp_up, w_exp_down):
    nb_c, seq_c, _ = x_prompt.shape
    nb_l, seq_l, _ = x_sample.shape
    tm = ROW_BLOCK

    wp, wkt = _pack_w_in(w_in)
    wk, wv = _pack_w_ukv(w_ukv)
    wo = w_out.astype(BF16)
    wfg, wfu, wfd = w_ffn_gate.astype(BF16), w_ffn_up.astype(BF16), w_ffn_down.astype(BF16)
    wegu = jnp.concatenate([w_exp_gate.astype(BF16), w_exp_up.astype(BF16)], axis=-1)
    wed = w_exp_down.astype(BF16)
    wr = jnp.pad(w_router, ((0, 0), (0, 0), (0, LANES - N_EXPERTS)))
    wrh = wr.astype(BF16)
    wr2 = jnp.concatenate([wrh, (wr - wrh.astype(F32)).astype(BF16)], axis=-1)
    qg = _pack_head_gain(q_norm)[:, None, :]
    kg = _pack_head_gain(k_norm)[:, None, :]
    kvg = kv_norm[:, None, :]
    lg = jax.nn.log_sigmoid(decay_logit.astype(F32))
    tabs_m = _rope_tables(seq_l, QK_ROPE_DIM)
    tabs_r = _rope_tables(seq_l, RET_DK)
    tabs_rt = tuple(t.T for t in tabs_r)
    cache_kr_pad = jnp.pad(cache_krope, ((0, 0), (0, 0), (0, 0), (0, LANES - QK_ROPE_DIM)))

    cs = jnp.concatenate([c_ctx[None, :], c, jnp.zeros((SUB - 1 - nb_l, D_MODEL), F32)], axis=0)
    mods = _adaln(cs, w_ada, b_ada).reshape(DEPTH, SUB, 6, 1, D_MODEL)

    yp = x_prompt.reshape(nb_c * seq_c, D_MODEL)
    ys = x_sample.reshape(nb_l * seq_l, D_MODEL)
    ckv_list, krope_list, state_list = [], [], []
    an, fn = attn_norm[:, None, :], ffn_norm[:, None, :]
    for l in range(DEPTH):
        for ctx in (True, False):
            if ctx:
                y, st = yp, _Stream(mods, l, 0, nb_c * seq_c // tm)
            else:
                y, st = ys, _Stream(mods, l, 1, seq_l // tm)
            q, ckv, kr, qr, kret, vr, g = _inproj(
                y, st, an, wp, wkt, qg, kvg, None if ctx else tabs_m + tabs_r + tabs_rt, tm)
            if ctx:
                a = _attn_ctx(q, ckv, kr, wk, wv, kg, l, seq_c)
                r, state = _retention(lg, l, qr, kret, vr, g, None, seq_c, RET_CTX_SEQS)
                ckv_list.append(ckv.reshape(nb_c, seq_c, KV_LORA_RANK))
                krope_list.append(kr[:, :QK_ROPE_DIM].reshape(nb_c, seq_c, QK_ROPE_DIM))
                state_list.append(state)
            else:
                a = _attn_lat(q, ckv, kr, cache_ckv, cache_kr_pad, wk, wv, kg, l, tabs_m, seq_l, Q_BLOCK)
                r, = _retention(lg, l, qr, kret, vr, g, state_ret, seq_l, 1)
            i = l // 2
            if l % 2 == 0:
                y = _ffn(y, a, r, st, wo, fn, wfg, wfu, wfd, i, tm)
            else:
                y = _moe(y, a, r, st, wo, fn, wr2, wegu, wed, i, tm, EXPERT_TILE)
            if ctx:
                yp = y
            else:
                ys = y
    return (yp.reshape(nb_c, seq_c, D_MODEL), ys.reshape(nb_l, seq_l, D_MODEL),
            jnp.stack(ckv_list, axis=1), jnp.stack(krope_list, axis=1), jnp.stack(state_list, axis=1))
```

```python
import functools
import math

import jax
import jax.numpy as jnp
import numpy as np
from jax import lax
from jax.experimental import pallas as pl
from jax.experimental.pallas import tpu as pltpu

D_MODEL = 1024
DEPTH = 4
GRID_W = 64
MLA_HEADS = 8
QK_NOPE_DIM = 64
QK_ROPE_DIM = 32
QK_HEAD_DIM = QK_NOPE_DIM + QK_ROPE_DIM
V_HEAD_DIM = 64
KV_LORA_RANK = 256
MLA_WIDTH = MLA_HEADS * V_HEAD_DIM
RET_HEADS = 4
RET_DK = 128
RET_DV = 128
RET_CHUNK = 128
RET_WIDTH = RET_HEADS * RET_DV
D_FF = 2816
N_EXPERTS = 8
EXPERT_FF = 1408
ROPE_BASE = 10000.0
EPS = 1e-6

LANES = 128
SUB = 8
QP_WIDTH = MLA_HEADS * LANES
OFF_Q = 0
OFF_CKV = OFF_Q + QP_WIDTH
OFF_KR = OFF_CKV + KV_LORA_RANK
OFF_QR = OFF_KR + LANES
OFF_VR = OFF_QR + RET_WIDTH
OFF_G = OFF_VR + RET_WIDTH
IN_COLS_P = OFF_G + RET_WIDTH

VMEM_LIMIT = 56 * 1024 * 1024
LOG2E = math.log2(math.e)

ROW_BLOCK = 512
DMA_ROW_BLOCK = 1024
Q_BLOCK = 512
EXPERT_TILE = 256
RET_CTX_SEQS = 4

F32 = jnp.float32
BF16 = jnp.bfloat16


def _dot(a, b):
    return jnp.dot(a, b, preferred_element_type=F32)


def _dot_nt(a, b):
    return lax.dot_general(a, b, (((1,), (1,)), ((), ())), preferred_element_type=F32)


def _dot_tn(a, b):
    return lax.dot_general(a, b, (((0,), (0,)), ((), ())), preferred_element_type=F32)


def _rms(x, n):
    return x * lax.rsqrt(jnp.sum(x * x, axis=-1, keepdims=True) * (1.0 / n) + EPS)


def _silu(x):
    return x * (1.0 / (1.0 + jnp.exp(-x)))


def _rope(t, cos, sa, sb, q):
    return t * cos + pltpu.roll(t, LANES - q, 1) * sa + pltpu.roll(t, q, 1) * sb


def _rope_t(t, cos, sa, sb, q):
    n = t.shape[0]
    up = jnp.concatenate([t[q:], t[:q]], axis=0)
    down = jnp.concatenate([t[n - q:], t[:n - q]], axis=0)
    return t * cos + up * sa + down * sb


def _params(sem, **kw):
    return pltpu.CompilerParams(dimension_semantics=sem, vmem_limit_bytes=VMEM_LIMIT, **kw)


def _layer(arr, idx, **kw):
    nd = arr.ndim
    return pl.BlockSpec((None,) + arr.shape[1:], lambda *g: (idx,) + (0,) * (nd - 1), **kw)


class _Stream:
    def __init__(self, mods, layer, base, blocks_per_entry):
        self.mods, self.layer, self.base, self.bpe = mods, layer, base, blocks_per_entry

    def mod(self, which):
        l, base, bpe = self.layer, self.base, self.bpe
        return pl.BlockSpec((None, None, None, 1, D_MODEL), lambda i, *_: (l, base + i // bpe, which, 0, 0))


def _adaln_body(c_ref, w_ref, b_ref, o_ref):
    x = _silu(c_ref[...]).astype(BF16)
    o_ref[0] = _dot(x, w_ref[0].astype(BF16)) + b_ref[0]


def _adaln(cs, w_ada, b_ada):
    tn = 1536
    nb = cs.shape[0]
    return pl.pallas_call(
        _adaln_body,
        grid=(DEPTH, 6 * D_MODEL // tn),
        in_specs=[
            pl.BlockSpec((nb, D_MODEL), lambda l, j: (0, 0)),
            pl.BlockSpec((1, D_MODEL, tn), lambda l, j: (l, 0, j)),
            pl.BlockSpec((1, 1, tn), lambda l, j: (l, 0, j)),
        ],
        out_specs=pl.BlockSpec((1, nb, tn), lambda l, j: (l, 0, j)),
        out_shape=jax.ShapeDtypeStruct((DEPTH, nb, 6 * D_MODEL), F32),
        compiler_params=_params(("arbitrary", "arbitrary")),
        name="adaln",
    )(cs, w_ada, b_ada.reshape(DEPTH, 1, 6 * D_MODEL))


def _inproj_body(rope, y_ref, gain_ref, sh_ref, sc_ref, w_ref, wkt_ref, qg_ref, kvg_ref, *rest):
    if rope:
        cm_ref, sam_ref, sbm_ref, cr_ref, sar_ref, sbr_ref, crt_ref, sart_ref, sbrt_ref = rest[:9]
        rest = rest[9:]
    q_ref, ckv_ref, kr_ref, qr_ref, kt_ref, vr_ref, g_ref = rest
    tm = y_ref.shape[0]

    half = tm // 2
    rows = [pl.ds(0, half), pl.ds(half, half)]

    hbs = []
    for rs in rows:
        h = _rms(y_ref[rs, :], D_MODEL) * gain_ref[...]
        hbs.append((h * (1.0 + sc_ref[...]) + sh_ref[...]).astype(BF16))

    prods = []
    for rs, hb in zip(rows, hbs):
        pq = _dot(hb, w_ref[:, OFF_Q:OFF_Q + QP_WIDTH])
        pckv = _dot(hb, w_ref[:, OFF_CKV:OFF_CKV + KV_LORA_RANK])
        kr_ref[rs, :] = _dot(hb, w_ref[:, OFF_KR:OFF_KR + LANES])
        pqr = _dot(hb, w_ref[:, OFF_QR:OFF_QR + RET_WIDTH])
        kt = _dot_nt(wkt_ref[...], hb)
        vr_ref[rs, :] = _dot(hb, w_ref[:, OFF_VR:OFF_VR + RET_WIDTH]).astype(BF16)
        g_ref[rs, :] = _dot(hb, w_ref[:, OFF_G:OFF_G + RET_WIDTH]).astype(BF16)
        prods.append((pq, pckv, pqr, kt))

    qg = qg_ref[...]
    for part, (rs, (pq, pckv, pqr, kt)) in enumerate(zip(rows, prods)):
        for hd in range(MLA_HEADS):
            t = _rms(pq[:, hd * LANES:(hd + 1) * LANES], QK_HEAD_DIM) * qg
            if rope:
                t = _rope(t, cm_ref[rs, :], sam_ref[rs, :], sbm_ref[rs, :], QK_ROPE_DIM // 4)
            q_ref[rs, hd * LANES:(hd + 1) * LANES] = t.astype(BF16)

        ckv_ref[rs, :] = _rms(pckv, KV_LORA_RANK) * kvg_ref[...]

        for hd in range(RET_HEADS):
            c0 = hd * LANES
            t = pqr[:, c0:c0 + LANES]
            if rope:
                t = _rope(t, cr_ref[rs, :], sar_ref[rs, :], sbr_ref[rs, :], RET_DK // 4)
            qr_ref[rs, c0:c0 + LANES] = t.astype(BF16)
        for hd in range(RET_HEADS):
            t = kt[hd * RET_DK:(hd + 1) * RET_DK, :]
            if rope:
                t = _rope_t(t, crt_ref[:, rs], sart_ref[:, rs], sbrt_ref[:, rs], RET_DK // 4)
            t = (t * (RET_DK ** -0.5)).astype(BF16)
            for j in range(half // RET_CHUNK):
                kt_ref[part * (half // RET_CHUNK) + j, hd * RET_DK:(hd + 1) * RET_DK, :] = (
                    t[:, j * RET_CHUNK:(j + 1) * RET_CHUNK])


def _inproj(y, st, gain, wp, wkt, qg, kvg, tables, tm):
    rows = y.shape[0]
    rope = tables is not None
    l, bpb = st.layer, st.bpe
    cpb = tm // RET_CHUNK
    in_specs = [
        pl.BlockSpec((tm, D_MODEL), lambda i: (i, 0)),
        _layer(gain, l), st.mod(0), st.mod(1), _layer(wp, l), _layer(wkt, l), _layer(qg, l), _layer(kvg, l),
    ]
    args = [y, gain, st.mods, st.mods, wp, wkt, qg, kvg]
    if rope:
        in_specs += [pl.BlockSpec((tm, LANES), lambda i: (i % bpb, 0))] * 6
        in_specs += [pl.BlockSpec((RET_DK, tm), lambda i: (0, i % bpb))] * 3
        args += list(tables)
    blocks = [(tm, QP_WIDTH), (tm, KV_LORA_RANK), (tm, LANES), (tm, RET_WIDTH),
              (cpb, RET_WIDTH, RET_CHUNK), (tm, RET_WIDTH), (tm, RET_WIDTH)]
    dtypes = (BF16, F32, F32, BF16, BF16, BF16, BF16)
    out_specs, out_shape = [], []
    for blk, dt in zip(blocks, dtypes):
        nd = len(blk)
        out_specs.append(pl.BlockSpec(blk, lambda i, nd=nd: (i,) + (0,) * (nd - 1)))
        out_shape.append(jax.ShapeDtypeStruct((rows // tm * blk[0],) + blk[1:], dt))
    return pl.pallas_call(
        functools.partial(_inproj_body, rope),
        grid=(rows // tm,),
        in_specs=in_specs,
        out_specs=out_specs,
        out_shape=out_shape,
        compiler_params=_params(("arbitrary",)),
        name="inproj_rope" if rope else "inproj",
    )(*args)


def _kv_up(ckv_f32, wk_ref, wv_ref):
    cb = ckv_f32.astype(BF16)
    return _dot(cb, wk_ref[...]), _dot(cb, wv_ref[...])


def _make_kv(up, kr_pad, kg, rope_tabs):
    kall, vall = up
    pe = kr_pad * kg
    if rope_tabs is not None:
        pe = _rope(pe, *rope_tabs, QK_ROPE_DIM // 4)
    ss_pe = jnp.sum(kr_pad * kr_pad, axis=-1, keepdims=True)
    ks, vs = [], []
    for hd in range(MLA_HEADS):
        kn = kall[:, hd * LANES:(hd + 1) * LANES]
        ss = jnp.sum(kn * kn, axis=-1, keepdims=True) + ss_pe
        k = (kn * kg + pe) * lax.rsqrt(ss * (1.0 / QK_HEAD_DIM) + EPS)
        ks.append(k.astype(BF16))
        vs.append(vall[:, hd * LANES:(hd + 1) * LANES].astype(BF16))
    return ks, vs


def _softmax_pv(s, v):
    m = jnp.max(s, axis=-1, keepdims=True)
    p = jnp.exp2((s - m) * (QK_HEAD_DIM ** -0.5 * LOG2E))
    l = jnp.sum(p, axis=-1, keepdims=True)
    return _dot(p.astype(BF16), v) * (1.0 / l)


def _attend(q_ref, get_k, get_v, o_ref):
    def scores(hd):
        return _dot_nt(q_ref[:, hd * LANES:(hd + 1) * LANES], get_k(hd))

    s_next = scores(0)
    acc = None
    for hd in range(MLA_HEADS):
        s = s_next
        if hd + 1 < MLA_HEADS:
            s_next = scores(hd + 1)
        o = _softmax_pv(s, get_v(hd))
        acc = o if hd % 2 == 0 else acc + o
        if hd % 2 == 1:
            o_ref[:, (hd // 2) * LANES:(hd // 2 + 1) * LANES] = acc.astype(BF16)


def _attn_ctx_body(q_ref, ckv_ref, kr_ref, wk_ref, wv_ref, kg_ref, o_ref):
    ks, vs = _make_kv(_kv_up(ckv_ref[...], wk_ref, wv_ref), kr_ref[...], kg_ref[...], None)
    _attend(q_ref, lambda hd: ks[hd], lambda hd: vs[hd], o_ref)


def _attn_ctx(q, ckv, kr, wk, wv, kg, l, seq):
    rows = q.shape[0]
    row = lambda b: (b, 0)
    return pl.pallas_call(
        _attn_ctx_body,
        grid=(rows // seq,),
        in_specs=[
            pl.BlockSpec((seq, QP_WIDTH), row),
            pl.BlockSpec((seq, KV_LORA_RANK), row),
            pl.BlockSpec((seq, LANES), row),
            _layer(wk, l), _layer(wv, l), _layer(kg, l),
        ],
        out_specs=pl.BlockSpec((seq, MLA_WIDTH), row),
        out_shape=jax.ShapeDtypeStruct((rows, MLA_WIDTH), BF16),
        compiler_params=_params(("arbitrary",)),
        name="attn_ctx",
    )(q, ckv, kr, wk, wv, kg)


def _attn_lat_body(past, seq, tq, q_ref, ckv_ref, kr_ref, cckv_ref, ckr_ref, wk_ref, wv_ref, kg_ref,
                   cm_ref, sam_ref, sbm_ref, o_ref, k_s, v_s):
    kc = 256

    @pl.when(pl.program_id(1) == 0)
    def _():
        kg = kg_ref[...]
        for c in range(past // kc):
            r = slice(c * kc, (c + 1) * kc)
            ks, vs = _make_kv(_kv_up(cckv_ref[r, :], wk_ref, wv_ref), ckr_ref[r, :], kg, None)
            for hd in range(MLA_HEADS):
                k_s[hd, r, :] = ks[hd]
                v_s[hd, r, :] = vs[hd]

        def chunk(c, carry):
            r0 = pl.multiple_of(c * kc, kc)
            rs = pl.ds(r0, kc)
            tabs = (cm_ref[rs, :], sam_ref[rs, :], sbm_ref[rs, :])
            ks, vs = _make_kv(_kv_up(ckv_ref[rs, :], wk_ref, wv_ref), kr_ref[rs, :], kg, tabs)
            ro = pl.ds(past + r0, kc)
            for hd in range(MLA_HEADS):
                k_s[hd, ro, :] = ks[hd]
                v_s[hd, ro, :] = vs[hd]
            return carry

        lax.fori_loop(0, seq // kc, chunk, 0)

    _attend(q_ref, lambda hd: k_s[hd], lambda hd: v_s[hd], o_ref)


def _attn_lat(q, ckv, kr, cache_ckv, cache_kr, wk, wv, kg, l, tabs_m, seq, tq):
    rows = q.shape[0]
    nb = rows // seq
    nq = seq // tq
    past = cache_ckv.shape[2]
    const = lambda b, i: (0, 0)
    return pl.pallas_call(
        functools.partial(_attn_lat_body, past, seq, tq),
        grid=(nb, nq),
        in_specs=[
            pl.BlockSpec((tq, QP_WIDTH), lambda b, i: (b * nq + i, 0)),
            pl.BlockSpec((seq, KV_LORA_RANK), lambda b, i: (b, 0)),
            pl.BlockSpec((seq, LANES), lambda b, i: (b, 0)),
            pl.BlockSpec((None, None, past, KV_LORA_RANK), lambda b, i: (b, l, 0, 0)),
            pl.BlockSpec((None, None, past, LANES), lambda b, i: (b, l, 0, 0)),
            _layer(wk, l), _layer(wv, l), _layer(kg, l),
            pl.BlockSpec((seq, LANES), const),
            pl.BlockSpec((seq, LANES), const),
            pl.BlockSpec((seq, LANES), const),
        ],
        out_specs=pl.BlockSpec((tq, MLA_WIDTH), lambda b, i: (b * nq + i, 0)),
        out_shape=jax.ShapeDtypeStruct((rows, MLA_WIDTH), BF16),
        scratch_shapes=[
            pltpu.VMEM((MLA_HEADS, past + seq, LANES), BF16),
            pltpu.VMEM((MLA_HEADS, past + seq, LANES), BF16),
        ],
        compiler_params=_params(("arbitrary", "arbitrary")),
        name="attn_lat",
    )(q, ckv, kr, cache_ckv, cache_kr, wk, wv, kg, *tabs_m)


def _ret_body(seq, nseq, has_state, layer, lg_ref, q_ref, kt_ref, v_ref, g_ref, *rest):
    if has_state:
        s0_ref, o_ref, acc_s, u_s, st_s, dec_s = rest
    else:
        o_ref, stout_ref, acc_s, u_s, st_s, dec_s = rest
    C = RET_CHUNK
    nc = seq // C
    ii = lax.broadcasted_iota(jnp.int32, (C, C), 0)
    jj = lax.broadcasted_iota(jnp.int32, (C, C), 1)
    diff = (ii - jj).astype(F32)
    pos = ii.astype(F32)
    post = jj.astype(F32)
    cfull = jnp.full((1, LANES), float(C), F32)
    cdf, cdb = [], []
    for hd in range(RET_HEADS):
        lgf = lg_ref[layer, 0, hd]
        lgb = lg_ref[layer, 1, hd]
        dec_s[hd, 0] = (jnp.where(diff >= 0, jnp.exp(lgf * jnp.maximum(diff, 0.0)), 0.0)
                        + jnp.where(diff <= 0, jnp.exp(lgb * jnp.maximum(-diff, 0.0)), 0.0))
        dec_s[hd, 1] = jnp.exp(lgf * (pos + 1.0))
        dec_s[hd, 2] = jnp.exp(lgb * (C - pos))
        dec_s[hd, 3] = jnp.exp(lgf * (C - 1.0 - post))
        dec_s[hd, 4] = jnp.exp(lgb * post)
        cdf.append(jnp.exp(lgf * cfull))
        cdb.append(jnp.exp(lgb * cfull))

    per_trip = 2
    items = [(sub, hd) for sub in range(per_trip) for hd in range(RET_HEADS)]
    cols = [slice(hd * LANES, (hd + 1) * LANES) for hd in range(RET_HEADS)]

    def p1(trip, carry):
        cs = [trip * per_trip + sub for sub in range(per_trip)]
        rs = [pl.ds(pl.multiple_of(c * C, C), C) for c in cs]
        kts = {(s, h): kt_ref[cs[s], h * RET_DK:(h + 1) * RET_DK, :] for s, h in items}
        vcs = {(s, h): v_ref[rs[s], cols[h]] for s, h in items}
        scores = {(s, h): _dot(q_ref[rs[s], cols[h]], kts[s, h]) for s, h in items}
        for s, h in items:
            kf = kts[s, h].astype(F32)
            kd = jnp.concatenate([(kf * dec_s[h, 3]).astype(BF16), (kf * dec_s[h, 4]).astype(BF16)], axis=0)
            u_s[cs[s], h] = _dot(kd, vcs[s, h])
        for s, h in items:
            acc_s[rs[s], cols[h]] = _dot((scores[s, h] * dec_s[h, 0]).astype(BF16), vcs[s, h])
        return carry

    lax.fori_loop(0, nseq * nc // per_trip, p1, 0)

    for sq in range(nseq):
        for hd in range(RET_HEADS):
            if has_state:
                init = (s0_ref[sq, 0, hd], s0_ref[sq, 1, hd])
            else:
                init = (jnp.zeros((RET_DK, RET_DV), F32), jnp.zeros((RET_DK, RET_DV), F32))

            def p2(t, carry, sq=sq, hd=hd):
                sf, sb = carry
                cf = sq * nc + t
                cb = sq * nc + (nc - 1 - t)
                st_s[cf, hd, :, 0:RET_DV] = sf.astype(BF16)
                st_s[cb, hd, :, RET_DV:2 * RET_DV] = sb.astype(BF16)
                sf = sf * cdf[hd] + u_s[cf, hd, 0:RET_DK, :]
                sb = sb * cdb[hd] + u_s[cb, hd, RET_DK:2 * RET_DK, :]
                return sf, sb

            sf, sb = lax.fori_loop(0, nc, p2, init)
            if not has_state:
                stout_ref[sq, 0, hd] = sf
                stout_ref[sq, 1, hd] = sb

    def p3(trip, carry):
        cs = [trip * per_trip + sub for sub in range(per_trip)]
        rs = [pl.ds(pl.multiple_of(c * C, C), C) for c in cs]
        crs = {(s, h): _dot(q_ref[rs[s], cols[h]], st_s[cs[s], h]) for s, h in items}
        for s, h in items:
            cr = crs[s, h]
            r = acc_s[rs[s], cols[h]] + cr[:, 0:RET_DV] * dec_s[h, 1] + cr[:, RET_DV:2 * RET_DV] * dec_s[h, 2]
            gh = g_ref[rs[s], cols[h]].astype(F32)
            o_ref[rs[s], cols[h]] = (_rms(r, RET_DV) * _silu(gh)).astype(BF16)
        return carry

    lax.fori_loop(0, nseq * nc // per_trip, p3, 0)


def _retention(lg, l, qr, kt, vr, g, s0, seq, nseq):
    rows = qr.shape[0]
    nb = rows // seq
    has_state = s0 is not None
    nct = nseq * seq // RET_CHUNK
    row = lambda b: (b, 0)
    if has_state:
        st_spec = pl.BlockSpec((nseq, None, 2, RET_HEADS, RET_DK, RET_DV), lambda b: (b, l, 0, 0, 0, 0))
    else:
        st_spec = pl.BlockSpec((nseq, 2, RET_HEADS, RET_DK, RET_DV), lambda b: (b, 0, 0, 0, 0))
    tok = pl.BlockSpec((nseq * seq, RET_WIDTH), row)
    in_specs = [pl.BlockSpec(memory_space=pltpu.SMEM), tok,
                pl.BlockSpec((nct, RET_WIDTH, RET_CHUNK), lambda b: (b, 0, 0)), tok, tok]
    args = [lg, qr, kt, vr, g]
    out_specs = [pl.BlockSpec((nseq * seq, RET_WIDTH), row)]
    out_shape = [jax.ShapeDtypeStruct((rows, RET_WIDTH), BF16)]
    if has_state:
        in_specs.append(st_spec)
        args.append(s0)
    else:
        out_specs.append(st_spec)
        out_shape.append(jax.ShapeDtypeStruct((nb, 2, RET_HEADS, RET_DK, RET_DV), F32))
    return pl.pallas_call(
        functools.partial(_ret_body, seq, nseq, has_state, l),
        grid=(nb // nseq,),
        in_specs=in_specs,
        out_specs=out_specs,
        out_shape=out_shape,
        scratch_shapes=[
            pltpu.VMEM((nseq * seq, RET_WIDTH), F32),
            pltpu.VMEM((nct, RET_HEADS, 2 * RET_DK, RET_DV), F32),
            pltpu.VMEM((nct, RET_HEADS, RET_DK, 2 * RET_DV), BF16),
            pltpu.VMEM((RET_HEADS, 5, RET_CHUNK, RET_CHUNK), F32),
        ],
        compiler_params=_params(("arbitrary",)),
        name="retention_lat" if has_state else "retention_ctx",
    )(*args)


def _mixer_residual(y_ref, a_ref, r_ref, wo_ref, g1_ref):
    out = _dot(a_ref[...], wo_ref[:MLA_WIDTH, :]) + _dot(r_ref[...], wo_ref[MLA_WIDTH:, :])
    return y_ref[...] + g1_ref[...] * out


def _ffn_in(y, gain_ref, sh_ref, sc_ref):
    h = _rms(y, D_MODEL) * gain_ref[...]
    return h * (1.0 + sc_ref[...]) + sh_ref[...]


def _ffn_body(y_ref, a_ref, r_ref, wo_ref, g1_ref, gain_ref, sh_ref, sc_ref, g2_ref, wg_ref, wu_ref, wd_ref, o_ref):
    tm = y_ref.shape[0]
    halves = [pl.ds(0, tm // 2), pl.ds(tm // 2, tm // 2)]
    mix = [_dot(a_ref[h, :], wo_ref[:MLA_WIDTH, :]) + _dot(r_ref[h, :], wo_ref[MLA_WIDTH:, :]) for h in halves]
    ys = [y_ref[h, :] + g1_ref[...] * m for h, m in zip(halves, mix)]
    hbs = [_ffn_in(y, gain_ref, sh_ref, sc_ref).astype(BF16) for y in ys]
    gus = [(_dot(hb, wg_ref[...]), _dot(hb, wu_ref[...])) for hb in hbs]
    acts = [(_silu(g) * u).astype(BF16) for g, u in gus]
    for h, y, act in zip(halves, ys, acts):
        o_ref[h, :] = y + g2_ref[...] * _dot(act, wd_ref[...])


def _ffn(y, a, r, st, wo, gain, wg, wu, wd, i, tm):
    rows = y.shape[0]
    l = st.layer
    row = lambda b: (b, 0)
    once = pl.Buffered(1)
    return pl.pallas_call(
        _ffn_body,
        grid=(rows // tm,),
        in_specs=[
            pl.BlockSpec((tm, D_MODEL), row),
            pl.BlockSpec((tm, MLA_WIDTH), row),
            pl.BlockSpec((tm, RET_WIDTH), row),
            _layer(wo, l, pipeline_mode=once), st.mod(2),
            _layer(gain, l), st.mod(3), st.mod(4), st.mod(5),
            _layer(wg, i, pipeline_mode=once), _layer(wu, i, pipeline_mode=once), _layer(wd, i, pipeline_mode=once),
        ],
        out_specs=pl.BlockSpec((tm, D_MODEL), row),
        out_shape=jax.ShapeDtypeStruct((rows, D_MODEL), F32),
        compiler_params=_params(("arbitrary",)),
        name="ffn",
    )(y, a, r, wo, st.mods, gain, st.mods, st.mods, st.mods, wg, wu, wd)


def _split_bf16(x):
    hi = x.astype(BF16)
    return hi, (x - hi.astype(F32)).astype(BF16)


def _route_body(y_ref, a_ref, r_ref, wo_ref, g1_ref, gain_ref, sh_ref, sc_ref, wr_ref,
                y1_ref, slot_ref, w_ref, cnt_ref, carry):
    tm = y_ref.shape[0]

    @pl.when(pl.program_id(0) == 0)
    def _():
        carry[...] = jnp.zeros_like(carry)

    y = _mixer_residual(y_ref, a_ref, r_ref, wo_ref, g1_ref)
    y1_ref[...] = y
    h = _ffn_in(y, gain_ref, sh_ref, sc_ref)
    hi, lo = _split_bf16(h)
    hh_hl = _dot(hi, wr_ref[...])
    logits = hh_hl[:, :LANES] + (_dot(lo, wr_ref[:, :LANES]) + hh_hl[:, LANES:])
    lane = lax.broadcasted_iota(jnp.int32, logits.shape, 1)
    ninf = jnp.float32(-jnp.inf)
    lg = jnp.where(lane < N_EXPERTS, logits, ninf)
    m1 = jnp.max(lg, axis=-1, keepdims=True)
    i1 = jnp.min(jnp.where(lg == m1, lane, LANES), axis=-1, keepdims=True)
    lg2 = jnp.where(lane == i1, ninf, lg)
    m2 = jnp.max(lg2, axis=-1, keepdims=True)
    i2 = jnp.min(jnp.where(lg2 == m2, lane, LANES), axis=-1, keepdims=True)
    ex = jnp.exp(m2 - m1)
    den = 1.0 / (1.0 + ex)
    w_ref[...] = jnp.where(lane == 0, den, 0.0) + jnp.where(lane == 1, ex * den, 0.0)

    sel1 = lane == i1
    sel2 = lane == i2
    onehot = jnp.where(sel1 | sel2, 1.0, 0.0)
    ri = lax.broadcasted_iota(jnp.int32, (tm, tm), 0)
    ci = lax.broadcasted_iota(jnp.int32, (tm, tm), 1)
    lower = jnp.where(ri > ci, 1.0, 0.0).astype(BF16)
    before = _dot(lower, onehot.astype(BF16)) + carry[...]
    r1 = jnp.sum(jnp.where(sel1, before, 0.0), axis=-1, keepdims=True)
    r2 = jnp.sum(jnp.where(sel2, before, 0.0), axis=-1, keepdims=True)
    slot_ref[...] = (jnp.where(lane == 0, i1, 0) + jnp.where(lane == 1, i2, 0)
                     + jnp.where(lane == 2, r1.astype(jnp.int32), 0)
                     + jnp.where(lane == 3, r2.astype(jnp.int32), 0))
    carry[...] += jnp.sum(onehot, axis=0, keepdims=True)
    cnt_ref[...] = carry[...].astype(jnp.int32)


def _route(y, a, r, st, wo, gain, wr, i, tm):
    rows = y.shape[0]
    l = st.layer
    row = lambda b: (b, 0)
    return pl.pallas_call(
        _route_body,
        grid=(rows // tm,),
        in_specs=[
            pl.BlockSpec((tm, D_MODEL), row),
            pl.BlockSpec((tm, MLA_WIDTH), row),
            pl.BlockSpec((tm, RET_WIDTH), row),
            _layer(wo, l), st.mod(2),
            _layer(gain, l), st.mod(3), st.mod(4),
            _layer(wr, i),
        ],
        out_specs=[pl.BlockSpec((tm, D_MODEL), row), pl.BlockSpec((tm, LANES), row),
                   pl.BlockSpec((tm, LANES), row), pl.BlockSpec((1, LANES), lambda b: (0, 0))],
        out_shape=[jax.ShapeDtypeStruct((rows, D_MODEL), F32), jax.ShapeDtypeStruct((rows, LANES), jnp.int32),
                   jax.ShapeDtypeStruct((rows, LANES), F32), jax.ShapeDtypeStruct((1, LANES), jnp.int32)],
        scratch_shapes=[pltpu.VMEM((1, LANES), F32)],
        compiler_params=_params(("arbitrary",)),
        name="route",
    )(y, a, r, wo, st.mods, gain, st.mods, st.mods, wr)


def _dispatch_body(tr, dest_ref, zt_ref, y_ref, gain_ref, sh_ref, sc_ref, xs_hbm, stage, zeros, sem, zsem):
    i = pl.program_id(0)
    tm = y_ref.shape[0]

    @pl.when(i == 0)
    def _():
        zeros[...] = jnp.zeros_like(zeros)

        def ztile(z):
            r0 = pl.multiple_of(zt_ref[z] * (tr * SUB), tr * SUB)
            return pltpu.make_async_copy(zeros, xs_hbm.at[pl.ds(r0, tr * SUB)], zsem)

        for z in range(zt_ref.shape[0]):
            @pl.when(zt_ref[z] >= 0)
            def _():
                ztile(z).start()
        for z in range(zt_ref.shape[0]):
            @pl.when(zt_ref[z] >= 0)
            def _():
                ztile(z).wait()

    h = _ffn_in(y_ref[...], gain_ref, sh_ref, sc_ref)
    for s in range(D_MODEL // LANES):
        stage[pl.ds(s, tm, stride=SUB), :] = h[:, s * LANES:(s + 1) * LANES]

    def body(t, c):
        src = stage.at[pl.ds(pl.multiple_of(t * SUB, SUB), SUB)]
        for k in range(2):
            dst = dest_ref[(i * tm + t) * 2 + k]
            pltpu.make_async_copy(src, xs_hbm.at[pl.ds(pl.multiple_of(dst * SUB, SUB), SUB)],
                                  sem).start(priority=k)
        return c

    lax.fori_loop(0, tm, body, 0, unroll=8)

    for k in range(2):
        pltpu.make_async_copy(stage, stage, sem).wait()


def _dispatch(dest_flat, ztiles, y, st, gain, tm, ntile, tr):
    rows = y.shape[0]
    grid_spec = pltpu.PrefetchScalarGridSpec(
        num_scalar_prefetch=2,
        grid=(rows // tm,),
        in_specs=[
            pl.BlockSpec((tm, D_MODEL), lambda i, dr, cr: (i, 0)),
            _layer(gain, st.layer), st.mod(3), st.mod(4),
        ],
        out_specs=pl.BlockSpec(memory_space=pl.ANY),
        scratch_shapes=[
            pltpu.VMEM((tm * SUB, LANES), F32),
            pltpu.VMEM((tr * SUB, LANES), F32),
            pltpu.SemaphoreType.DMA(()),
            pltpu.SemaphoreType.DMA(()),
        ],
    )
    return pl.pallas_call(
        functools.partial(_dispatch_body, tr),
        grid_spec=grid_spec,
        out_shape=jax.ShapeDtypeStruct((ntile * tr * SUB, LANES), F32),
        compiler_params=_params(("arbitrary",), disable_bounds_checks=True),
        name="dispatch",
    )(dest_flat, ztiles, y, gain, st.mods, st.mods)


def _experts_body(tr, te_ref, nu_ref, xs_ref, wg_ref, wu_ref, wd_ref, o_ref):
    nt = D_MODEL // LANES
    used = pl.program_id(0) < nu_ref[0]

    @pl.when(used)
    def _():
        x = jnp.concatenate([xs_ref[pl.ds(s, tr, stride=SUB), :] for s in range(nt)], axis=1).astype(BF16)
        a = (_silu(_dot(x, wg_ref[0])) * _dot(x, wu_ref[0])).astype(BF16)
        o = _dot(a, wd_ref[0])
        for s in range(nt):
            o_ref[pl.ds(s, tr, stride=SUB), :] = o[:, s * LANES:(s + 1) * LANES]

    @pl.when(jnp.logical_not(used))
    def _():
        o_ref[...] = jnp.zeros_like(o_ref)


def _experts(tile_e, nused, xs, wg, wu, wd, i, tr):
    ntile = tile_e.shape[0]
    blk = lambda j, te, nu: (j, 0)
    wsel = lambda j, te, nu: (i, te[j], 0, 0)
    grid_spec = pltpu.PrefetchScalarGridSpec(
        num_scalar_prefetch=2,
        grid=(ntile,),
        in_specs=[
            pl.BlockSpec((tr * SUB, LANES), blk),
            pl.BlockSpec((None, 1, D_MODEL, EXPERT_FF), wsel),
            pl.BlockSpec((None, 1, D_MODEL, EXPERT_FF), wsel),
            pl.BlockSpec((None, 1, EXPERT_FF, D_MODEL), wsel),
        ],
        out_specs=pl.BlockSpec((tr * SUB, LANES), blk),
    )
    return pl.pallas_call(
        functools.partial(_experts_body, tr),
        grid_spec=grid_spec,
        out_shape=jax.ShapeDtypeStruct(xs.shape, F32),
        compiler_params=_params(("arbitrary",)),
        name="experts",
    )(tile_e, nused, xs, wg, wu, wd)


def _tile_plan(cnt, slots, ntile, tr):
    ptiles = (cnt + tr - 1) // tr
    tile_end = jnp.cumsum(ptiles)
    nused = tile_end[-1]
    off = (tile_end - ptiles) * tr
    experts = jnp.arange(N_EXPERTS, dtype=jnp.int32)
    dest = slots[:, 2:4] + jnp.sum(jnp.where(slots[:, 0:2, None] == experts, off, 0), axis=-1)
    j = jnp.minimum(jnp.arange(ntile), nused - 1)
    te = jnp.sum((j[:, None] >= tile_end[None, :]).astype(jnp.int32), axis=1)
    last = jnp.where(ptiles > 0, tile_end - 1, -1)
    tail = nused + jnp.arange(ntile - slots.shape[0] * 2 // tr)
    ztiles = jnp.concatenate([last, jnp.where(tail < ntile, tail, -1)])
    i32 = lambda a: a.astype(jnp.int32)
    return i32(dest).reshape(-1), i32(te), i32(nused).reshape(1), i32(ztiles)


def _combine_body(dest_ref, y_ref, g2_ref, w_ref, y2_hbm, o_ref, stage, sem):
    i = pl.program_id(0)
    tm = y_ref.shape[0]

    def body(t, c):
        for k in range(2):
            src = dest_ref[(i * tm + t) * 2 + k]
            pltpu.make_async_copy(y2_hbm.at[pl.ds(pl.multiple_of(src * SUB, SUB), SUB)],
                                  stage.at[k, pl.ds(pl.multiple_of(t * SUB, SUB), SUB)],
                                  sem).start(priority=k)
        return c

    lax.fori_loop(0, tm, body, 0, unroll=8)
    w = w_ref[...]
    w1 = w[:, 0:1]
    w2 = w[:, 1:2]
    for k in range(2):
        pltpu.make_async_copy(stage.at[k], stage.at[k], sem).wait()
    parts = []
    for s in range(D_MODEL // LANES):
        parts.append(w1 * stage[0, pl.ds(s, tm, stride=SUB), :] + w2 * stage[1, pl.ds(s, tm, stride=SUB), :])
    o_ref[...] = y_ref[...] + g2_ref[...] * jnp.concatenate(parts, axis=1)


def _combine(dest_flat, y, st, w, y2, tm):
    rows = y.shape[0]
    grid_spec = pltpu.PrefetchScalarGridSpec(
        num_scalar_prefetch=1,
        grid=(rows // tm,),
        in_specs=[
            pl.BlockSpec((tm, D_MODEL), lambda i, dr: (i, 0)),
            st.mod(5),
            pl.BlockSpec((tm, LANES), lambda i, dr: (i, 0)),
            pl.BlockSpec(memory_space=pl.ANY),
        ],
        out_specs=pl.BlockSpec((tm, D_MODEL), lambda i, dr: (i, 0)),
        scratch_shapes=[
            pltpu.VMEM((2, tm * SUB, LANES), F32),
            pltpu.SemaphoreType.DMA(()),
        ],
    )
    return pl.pallas_call(
        _combine_body,
        grid_spec=grid_spec,
        out_shape=jax.ShapeDtypeStruct((rows, D_MODEL), F32),
        compiler_params=_params(("arbitrary",), disable_bounds_checks=True),
        name="combine",
    )(dest_flat, y, st.mods, w, y2)


def _moe(y, a, r, st, wo, gain, wr, wg, wu, wd, i, tm, tr):
    rows = y.shape[0]
    ntile = 2 * rows // tr + N_EXPERTS
    y1, slots, w, cnt = _route(y, a, r, st, wo, gain, wr, i, tm)
    dest_flat, te, nused, ztiles = _tile_plan(cnt[0, :N_EXPERTS], slots[:, :4], ntile, tr)
    st_dma = _Stream(st.mods, st.layer, st.base, st.bpe * tm // DMA_ROW_BLOCK)
    xs = _dispatch(dest_flat, ztiles, y1, st_dma, gain, DMA_ROW_BLOCK, ntile, tr)
    y2 = _experts(te, nused, xs, wg, wu, wd, i, tr)
    return _combine(dest_flat, y1, st_dma, w, y2, DMA_ROW_BLOCK)


def _rope_tables(n_tok, dim):
    rows = n_tok // GRID_W
    row = jnp.repeat(jnp.arange(rows), GRID_W)
    col = jnp.tile(jnp.arange(GRID_W), rows)
    half = dim // 2
    freqs = ROPE_BASE ** (-jnp.arange(0, half, 2, dtype=F32) / half)

    def ang(p):
        a = p.astype(F32)[:, None] * freqs[None, :]
        return jnp.concatenate([a, a], axis=-1)

    angles = jnp.concatenate([ang(row), ang(col)], axis=-1)
    cos, sin = jnp.cos(angles), jnp.sin(angles)
    first = (np.arange(dim) % half) < (half // 2)
    sa = jnp.where(first, -sin, 0.0)
    sb = jnp.where(first, 0.0, sin)
    pad = LANES - dim
    cos = jnp.pad(cos, ((0, 0), (0, pad)), constant_values=1.0)
    sa = jnp.pad(sa, ((0, 0), (0, pad)))
    sb = jnp.pad(sb, ((0, 0), (0, pad)))
    return cos, sa, sb


def _pack_w_in(w_in):
    zeros = jnp.zeros(w_in.shape[:-1] + (LANES - QK_HEAD_DIM,), w_in.dtype)
    parts = []
    for hd in range(MLA_HEADS):
        b = hd * QK_HEAD_DIM
        parts += [w_in[..., b + QK_NOPE_DIM:b + QK_HEAD_DIM], w_in[..., b:b + QK_NOPE_DIM], zeros]
    o = MLA_HEADS * QK_HEAD_DIM
    parts.append(w_in[..., o:o + KV_LORA_RANK])
    o += KV_LORA_RANK
    parts += [w_in[..., o:o + QK_ROPE_DIM],
              jnp.zeros(w_in.shape[:-1] + (LANES - QK_ROPE_DIM,), w_in.dtype)]
    o += QK_ROPE_DIM
    parts += [w_in[..., o:o + RET_WIDTH], w_in[..., o + 2 * RET_WIDTH:]]
    w_kret_t = jnp.swapaxes(w_in[..., o + RET_WIDTH:o + 2 * RET_WIDTH], -1, -2)
    return jnp.concatenate(parts, axis=-1).astype(BF16), w_kret_t.astype(BF16)


def _pack_w_ukv(w_ukv):
    lead = w_ukv.shape[:-1]
    z32 = jnp.zeros(lead + (QK_ROPE_DIM,), w_ukv.dtype)
    z64 = jnp.zeros(lead + (V_HEAD_DIM,), w_ukv.dtype)
    kp, vp = [], []
    for hd in range(MLA_HEADS):
        b = hd * (QK_NOPE_DIM + V_HEAD_DIM)
        kp += [z32, w_ukv[..., b:b + QK_NOPE_DIM], z32]
        v = w_ukv[..., b + QK_NOPE_DIM:b + QK_NOPE_DIM + V_HEAD_DIM]
        vp += [v, z64] if hd % 2 == 0 else [z64, v]
    return jnp.concatenate(kp, axis=-1).astype(BF16), jnp.concatenate(vp, axis=-1).astype(BF16)


def _pack_head_gain(g):
    z = jnp.zeros(g.shape[:-1] + (LANES - QK_HEAD_DIM,), g.dtype)
    return jnp.concatenate([g[..., QK_NOPE_DIM:], g[..., :QK_NOPE_DIM], z], axis=-1)


def kernel(x_prompt, x_sample, cache_ckv, cache_krope, state_ret, c, c_ctx, attn_norm, ffn_norm, w_ada, b_ada, w_in, kv_norm, w_ukv, q_norm, k_norm, decay_logit, w_out, w_ffn_gate, w_ffn_up, w_ffn_down, w_router, w_exp_gate, w_exp_up, w_exp_down):
    nb_c, seq_c, _ = x_prompt.shape
    nb_l, seq_l, _ = x_sample.shape
    tm = ROW_BLOCK

    wp, wkt = _pack_w_in(w_in)
    wk, wv = _pack_w_ukv(w_ukv)
    wo = w_out.astype(BF16)
    wfg, wfu, wfd = w_ffn_gate.astype(BF16), w_ffn_up.astype(BF16), w_ffn_down.astype(BF16)
    weg, weu, wed = w_exp_gate.astype(BF16), w_exp_up.astype(BF16), w_exp_down.astype(BF16)
    wr = jnp.pad(w_router, ((0, 0), (0, 0), (0, LANES - N_EXPERTS)))
    wrh = wr.astype(BF16)
    wr2 = jnp.concatenate([wrh, (wr - wrh.astype(F32)).astype(BF16)], axis=-1)
    qg = _pack_head_gain(q_norm)[:, None, :]
    kg = _pack_head_gain(k_norm)[:, None, :]
    kvg = kv_norm[:, None, :]
    lg = jax.nn.log_sigmoid(decay_logit.astype(F32))
    tabs_m = _rope_tables(seq_l, QK_ROPE_DIM)
    tabs_r = _rope_tables(seq_l, RET_DK)
    tabs_rt = tuple(t.T for t in tabs_r)
    cache_kr_pad = jnp.pad(cache_krope, ((0, 0), (0, 0), (0, 0), (0, LANES - QK_ROPE_DIM)))

    cs = jnp.concatenate([c_ctx[None, :], c, jnp.zeros((SUB - 1 - nb_l, D_MODEL), F32)], axis=0)
    mods = _adaln(cs, w_ada, b_ada).reshape(DEPTH, SUB, 6, 1, D_MODEL)

    yp = x_prompt.reshape(nb_c * seq_c, D_MODEL)
    ys = x_sample.reshape(nb_l * seq_l, D_MODEL)
    ckv_list, krope_list, state_list = [], [], []
    an, fn = attn_norm[:, None, :], ffn_norm[:, None, :]
    for l in range(DEPTH):
        for ctx in (True, False):
            if ctx:
                y, st = yp, _Stream(mods, l, 0, nb_c * seq_c // tm)
            else:
                y, st = ys, _Stream(mods, l, 1, seq_l // tm)
            q, ckv, kr, qr, kret, vr, g = _inproj(
                y, st, an, wp, wkt, qg, kvg, None if ctx else tabs_m + tabs_r + tabs_rt, tm)
            if ctx:
                a = _attn_ctx(q, ckv, kr, wk, wv, kg, l, seq_c)
                r, state = _retention(lg, l, qr, kret, vr, g, None, seq_c, RET_CTX_SEQS)
                ckv_list.append(ckv.reshape(nb_c, seq_c, KV_LORA_RANK))
                krope_list.append(kr[:, :QK_ROPE_DIM].reshape(nb_c, seq_c, QK_ROPE_DIM))
                state_list.append(state)
            else:
                a = _attn_lat(q, ckv, kr, cache_ckv, cache_kr_pad, wk, wv, kg, l, tabs_m, seq_l, Q_BLOCK)
                r, = _retention(lg, l, qr, kret, vr, g, state_ret, seq_l, 1)
            i = l // 2
            if l % 2 == 0:
                y = _ffn(y, a, r, st, wo, fn, wfg, wfu, wfd, i, tm)
            else:
                y = _moe(y, a, r, st, wo, fn, wr2, weg, weu, wed, i, tm, EXPERT_TILE)
            if ctx:
                yp = y
            else:
                ys = y
    return (yp.reshape(nb_c, seq_c, D_MODEL), ys.reshape(nb_l, seq_l, D_MODEL),
            jnp.stack(ckv_list, axis=1), jnp.stack(krope_list, axis=1), jnp.stack(state_list, axis=1))
```

```python
import functools
import math

import jax
import jax.numpy as jnp
import numpy as np
from jax import lax
from jax.experimental import pallas as pl
from jax.experimental.pallas import tpu as pltpu

D_MODEL = 1024
DEPTH = 4
GRID_W = 64
MLA_HEADS = 8
QK_NOPE_DIM = 64
QK_ROPE_DIM = 32
QK_HEAD_DIM = QK_NOPE_DIM + QK_ROPE_DIM
V_HEAD_DIM = 64
KV_LORA_RANK = 256
MLA_WIDTH = MLA_HEADS * V_HEAD_DIM
RET_HEADS = 4
RET_DK = 128
RET_DV = 128
RET_CHUNK = 128
RET_WIDTH = RET_HEADS * RET_DV
D_FF = 2816
N_EXPERTS = 8
EXPERT_FF = 1408
ROPE_BASE = 10000.0
EPS = 1e-6

LANES = 128
SUB = 8
QP_WIDTH = MLA_HEADS * LANES
OFF_Q = 0
OFF_CKV = OFF_Q + QP_WIDTH
OFF_KR = OFF_CKV + KV_LORA_RANK
OFF_QR = OFF_KR + LANES
OFF_VR = OFF_QR + RET_WIDTH
OFF_G = OFF_VR + RET_WIDTH
IN_COLS_P = OFF_G + RET_WIDTH

VMEM_LIMIT = 56 * 1024 * 1024
LOG2E = math.log2(math.e)

ROW_BLOCK = 512
DMA_ROW_BLOCK = 1024
Q_BLOCK = 512
EXPERT_TILE = 256
RET_CTX_SEQS = 4

F32 = jnp.float32
BF16 = jnp.bfloat16


def _dot(a, b):
    return jnp.dot(a, b, preferred_element_type=F32)


def _dot_nt(a, b):
    return lax.dot_general(a, b, (((1,), (1,)), ((), ())), preferred_element_type=F32)


def _dot_tn(a, b):
    return lax.dot_general(a, b, (((0,), (0,)), ((), ())), preferred_element_type=F32)


def _rms(x, n):
    return x * lax.rsqrt(jnp.sum(x * x, axis=-1, keepdims=True) * (1.0 / n) + EPS)


def _silu(x):
    return x * (1.0 / (1.0 + jnp.exp(-x)))


def _rope(t, cos, sa, sb, q):
    return t * cos + pltpu.roll(t, LANES - q, 1) * sa + pltpu.roll(t, q, 1) * sb


def _rope_t(t, cos, sa, sb, q):
    n = t.shape[0]
    up = jnp.concatenate([t[q:], t[:q]], axis=0)
    down = jnp.concatenate([t[n - q:], t[:n - q]], axis=0)
    return t * cos + up * sa + down * sb


def _params(sem, **kw):
    return pltpu.CompilerParams(dimension_semantics=sem, vmem_limit_bytes=VMEM_LIMIT, **kw)


def _layer(arr, idx, **kw):
    nd = arr.ndim
    return pl.BlockSpec((None,) + arr.shape[1:], lambda *g: (idx,) + (0,) * (nd - 1), **kw)


class _Stream:
    def __init__(self, mods, layer, base, blocks_per_entry):
        self.mods, self.layer, self.base, self.bpe = mods, layer, base, blocks_per_entry

    def mod(self, which):
        l, base, bpe = self.layer, self.base, self.bpe
        return pl.BlockSpec((None, None, None, 1, D_MODEL), lambda i, *_: (l, base + i // bpe, which, 0, 0))


def _adaln_body(c_ref, w_ref, b_ref, o_ref):
    x = _silu(c_ref[...]).astype(BF16)
    o_ref[0] = _dot(x, w_ref[0].astype(BF16)) + b_ref[0]


def _adaln(cs, w_ada, b_ada):
    tn = 1536
    nb = cs.shape[0]
    return pl.pallas_call(
        _adaln_body,
        grid=(DEPTH, 6 * D_MODEL // tn),
        in_specs=[
            pl.BlockSpec((nb, D_MODEL), lambda l, j: (0, 0)),
            pl.BlockSpec((1, D_MODEL, tn), lambda l, j: (l, 0, j)),
            pl.BlockSpec((1, 1, tn), lambda l, j: (l, 0, j)),
        ],
        out_specs=pl.BlockSpec((1, nb, tn), lambda l, j: (l, 0, j)),
        out_shape=jax.ShapeDtypeStruct((DEPTH, nb, 6 * D_MODEL), F32),
        compiler_params=_params(("arbitrary", "arbitrary")),
        name="adaln",
    )(cs, w_ada, b_ada.reshape(DEPTH, 1, 6 * D_MODEL))


def _inproj_body(rope, y_ref, gain_ref, sh_ref, sc_ref, w_ref, wkt_ref, qg_ref, kvg_ref, *rest):
    if rope:
        cm_ref, sam_ref, sbm_ref, cr_ref, sar_ref, sbr_ref, crt_ref, sart_ref, sbrt_ref = rest[:9]
        rest = rest[9:]
    q_ref, ckv_ref, kr_ref, qr_ref, kt_ref, vr_ref, g_ref = rest
    tm = y_ref.shape[0]

    half = tm // 2
    rows = [pl.ds(0, half), pl.ds(half, half)]

    hbs = []
    for rs in rows:
        h = _rms(y_ref[rs, :], D_MODEL) * gain_ref[...]
        hbs.append((h * (1.0 + sc_ref[...]) + sh_ref[...]).astype(BF16))

    prods = []
    for rs, hb in zip(rows, hbs):
        pq = _dot(hb, w_ref[:, OFF_Q:OFF_Q + QP_WIDTH])
        pckv = _dot(hb, w_ref[:, OFF_CKV:OFF_CKV + KV_LORA_RANK])
        kr_ref[rs, :] = _dot(hb, w_ref[:, OFF_KR:OFF_KR + LANES])
        pqr = _dot(hb, w_ref[:, OFF_QR:OFF_QR + RET_WIDTH])
        kt = _dot_nt(wkt_ref[...], hb)
        vr_ref[rs, :] = _dot(hb, w_ref[:, OFF_VR:OFF_VR + RET_WIDTH]).astype(BF16)
        g_ref[rs, :] = _dot(hb, w_ref[:, OFF_G:OFF_G + RET_WIDTH]).astype(BF16)
        prods.append((pq, pckv, pqr, kt))

    qg = qg_ref[...]
    for part, (rs, (pq, pckv, pqr, kt)) in enumerate(zip(rows, prods)):
        for hd in range(MLA_HEADS):
            t = _rms(pq[:, hd * LANES:(hd + 1) * LANES], QK_HEAD_DIM) * qg
            if rope:
                t = _rope(t, cm_ref[rs, :], sam_ref[rs, :], sbm_ref[rs, :], QK_ROPE_DIM // 4)
            q_ref[rs, hd * LANES:(hd + 1) * LANES] = t.astype(BF16)

        ckv_ref[rs, :] = _rms(pckv, KV_LORA_RANK) * kvg_ref[...]

        for hd in range(RET_HEADS):
            c0 = hd * LANES
            t = pqr[:, c0:c0 + LANES]
            if rope:
                t = _rope(t, cr_ref[rs, :], sar_ref[rs, :], sbr_ref[rs, :], RET_DK // 4)
            qr_ref[rs, c0:c0 + LANES] = t.astype(BF16)
        for hd in range(RET_HEADS):
            t = kt[hd * RET_DK:(hd + 1) * RET_DK, :]
            if rope:
                t = _rope_t(t, crt_ref[:, rs], sart_ref[:, rs], sbrt_ref[:, rs], RET_DK // 4)
            t = (t * (RET_DK ** -0.5)).astype(BF16)
            for j in range(half // RET_CHUNK):
                kt_ref[part * (half // RET_CHUNK) + j, hd * RET_DK:(hd + 1) * RET_DK, :] = (
                    t[:, j * RET_CHUNK:(j + 1) * RET_CHUNK])


def _inproj(y, st, gain, wp, wkt, qg, kvg, tables, tm):
    rows = y.shape[0]
    rope = tables is not None
    l, bpb = st.layer, st.bpe
    cpb = tm // RET_CHUNK
    in_specs = [
        pl.BlockSpec((tm, D_MODEL), lambda i: (i, 0)),
        _layer(gain, l), st.mod(0), st.mod(1), _layer(wp, l), _layer(wkt, l), _layer(qg, l), _layer(kvg, l),
    ]
    args = [y, gain, st.mods, st.mods, wp, wkt, qg, kvg]
    if rope:
        in_specs += [pl.BlockSpec((tm, LANES), lambda i: (i % bpb, 0))] * 6
        in_specs += [pl.BlockSpec((RET_DK, tm), lambda i: (0, i % bpb))] * 3
        args += list(tables)
    blocks = [(tm, QP_WIDTH), (tm, KV_LORA_RANK), (tm, LANES), (tm, RET_WIDTH),
              (cpb, RET_WIDTH, RET_CHUNK), (tm, RET_WIDTH), (tm, RET_WIDTH)]
    dtypes = (BF16, F32, F32, BF16, BF16, BF16, BF16)
    out_specs, out_shape = [], []
    for blk, dt in zip(blocks, dtypes):
        nd = len(blk)
        out_specs.append(pl.BlockSpec(blk, lambda i, nd=nd: (i,) + (0,) * (nd - 1)))
        out_shape.append(jax.ShapeDtypeStruct((rows // tm * blk[0],) + blk[1:], dt))
    return pl.pallas_call(
        functools.partial(_inproj_body, rope),
        grid=(rows // tm,),
        in_specs=in_specs,
        out_specs=out_specs,
        out_shape=out_shape,
        compiler_params=_params(("arbitrary",)),
        name="inproj_rope" if rope else "inproj",
    )(*args)


def _kv_up(ckv_f32, wk_ref, wv_ref):
    cb = ckv_f32.astype(BF16)
    return _dot(cb, wk_ref[...]), _dot(cb, wv_ref[...])


def _make_kv(up, kr_pad, kg, rope_tabs):
    kall, vall = up
    pe = kr_pad * kg
    if rope_tabs is not None:
        pe = _rope(pe, *rope_tabs, QK_ROPE_DIM // 4)
    ss_pe = jnp.sum(kr_pad * kr_pad, axis=-1, keepdims=True)
    ks, vs = [], []
    for hd in range(MLA_HEADS):
        kn = kall[:, hd * LANES:(hd + 1) * LANES]
        ss = jnp.sum(kn * kn, axis=-1, keepdims=True) + ss_pe
        k = (kn * kg + pe) * lax.rsqrt(ss * (1.0 / QK_HEAD_DIM) + EPS)
        ks.append(k.astype(BF16))
        vs.append(vall[:, hd * LANES:(hd + 1) * LANES].astype(BF16))
    return ks, vs


def _softmax_pv(s, v):
    m = jnp.max(s, axis=-1, keepdims=True)
    p = jnp.exp2((s - m) * (QK_HEAD_DIM ** -0.5 * LOG2E))
    l = jnp.sum(p, axis=-1, keepdims=True)
    return _dot(p.astype(BF16), v) * (1.0 / l)


def _attend(q_ref, get_k, get_v, o_ref):
    def scores(hd):
        return _dot_nt(q_ref[:, hd * LANES:(hd + 1) * LANES], get_k(hd))

    s_next = scores(0)
    acc = None
    for hd in range(MLA_HEADS):
        s = s_next
        if hd + 1 < MLA_HEADS:
            s_next = scores(hd + 1)
        o = _softmax_pv(s, get_v(hd))
        acc = o if hd % 2 == 0 else acc + o
        if hd % 2 == 1:
            o_ref[:, (hd // 2) * LANES:(hd // 2 + 1) * LANES] = acc.astype(BF16)


def _attn_ctx_body(q_ref, ckv_ref, kr_ref, wk_ref, wv_ref, kg_ref, o_ref):
    ks, vs = _make_kv(_kv_up(ckv_ref[...], wk_ref, wv_ref), kr_ref[...], kg_ref[...], None)
    _attend(q_ref, lambda hd: ks[hd], lambda hd: vs[hd], o_ref)


def _attn_ctx(q, ckv, kr, wk, wv, kg, l, seq):
    rows = q.shape[0]
    row = lambda b: (b, 0)
    return pl.pallas_call(
        _attn_ctx_body,
        grid=(rows // seq,),
        in_specs=[
            pl.BlockSpec((seq, QP_WIDTH), row),
            pl.BlockSpec((seq, KV_LORA_RANK), row),
            pl.BlockSpec((seq, LANES), row),
            _layer(wk, l), _layer(wv, l), _layer(kg, l),
        ],
        out_specs=pl.BlockSpec((seq, MLA_WIDTH), row),
        out_shape=jax.ShapeDtypeStruct((rows, MLA_WIDTH), BF16),
        compiler_params=_params(("arbitrary",)),
        name="attn_ctx",
    )(q, ckv, kr, wk, wv, kg)


def _attn_lat_body(past, seq, tq, q_ref, ckv_ref, kr_ref, cckv_ref, ckr_ref, wk_ref, wv_ref, kg_ref,
                   cm_ref, sam_ref, sbm_ref, o_ref, k_s, v_s):
    kc = 256

    @pl.when(pl.program_id(1) == 0)
    def _():
        kg = kg_ref[...]
        for c in range(past // kc):
            r = slice(c * kc, (c + 1) * kc)
            ks, vs = _make_kv(_kv_up(cckv_ref[r, :], wk_ref, wv_ref), ckr_ref[r, :], kg, None)
            for hd in range(MLA_HEADS):
                k_s[hd, r, :] = ks[hd]
                v_s[hd, r, :] = vs[hd]

        def chunk(c, carry):
            r0 = pl.multiple_of(c * kc, kc)
            rs = pl.ds(r0, kc)
            tabs = (cm_ref[rs, :], sam_ref[rs, :], sbm_ref[rs, :])
            ks, vs = _make_kv(_kv_up(ckv_ref[rs, :], wk_ref, wv_ref), kr_ref[rs, :], kg, tabs)
            ro = pl.ds(past + r0, kc)
            for hd in range(MLA_HEADS):
                k_s[hd, ro, :] = ks[hd]
                v_s[hd, ro, :] = vs[hd]
            return carry

        lax.fori_loop(0, seq // kc, chunk, 0)

    _attend(q_ref, lambda hd: k_s[hd], lambda hd: v_s[hd], o_ref)


def _attn_lat(q, ckv, kr, cache_ckv, cache_kr, wk, wv, kg, l, tabs_m, seq, tq):
    rows = q.shape[0]
    nb = rows // seq
    nq = seq // tq
    past = cache_ckv.shape[2]
    const = lambda b, i: (0, 0)
    return pl.pallas_call(
        functools.partial(_attn_lat_body, past, seq, tq),
        grid=(nb, nq),
        in_specs=[
            pl.BlockSpec((tq, QP_WIDTH), lambda b, i: (b * nq + i, 0)),
            pl.BlockSpec((seq, KV_LORA_RANK), lambda b, i: (b, 0)),
            pl.BlockSpec((seq, LANES), lambda b, i: (b, 0)),
            pl.BlockSpec((None, None, past, KV_LORA_RANK), lambda b, i: (b, l, 0, 0)),
            pl.BlockSpec((None, None, past, LANES), lambda b, i: (b, l, 0, 0)),
            _layer(wk, l), _layer(wv, l), _layer(kg, l),
            pl.BlockSpec((seq, LANES), const),
            pl.BlockSpec((seq, LANES), const),
            pl.BlockSpec((seq, LANES), const),
        ],
        out_specs=pl.BlockSpec((tq, MLA_WIDTH), lambda b, i: (b * nq + i, 0)),
        out_shape=jax.ShapeDtypeStruct((rows, MLA_WIDTH), BF16),
        scratch_shapes=[
            pltpu.VMEM((MLA_HEADS, past + seq, LANES), BF16),
            pltpu.VMEM((MLA_HEADS, past + seq, LANES), BF16),
        ],
        compiler_params=_params(("arbitrary", "arbitrary")),
        name="attn_lat",
    )(q, ckv, kr, cache_ckv, cache_kr, wk, wv, kg, *tabs_m)


def _ret_body(seq, nseq, has_state, layer, lg_ref, q_ref, kt_ref, v_ref, g_ref, *rest):
    if has_state:
        s0_ref, o_ref, acc_s, u_s, st_s, dec_s = rest
    else:
        o_ref, stout_ref, acc_s, u_s, st_s, dec_s = rest
    C = RET_CHUNK
    nc = seq // C
    ii = lax.broadcasted_iota(jnp.int32, (C, C), 0)
    jj = lax.broadcasted_iota(jnp.int32, (C, C), 1)
    diff = (ii - jj).astype(F32)
    pos = ii.astype(F32)
    post = jj.astype(F32)
    cfull = jnp.full((1, LANES), float(C), F32)
    cdf, cdb = [], []
    for hd in range(RET_HEADS):
        lgf = lg_ref[layer, 0, hd]
        lgb = lg_ref[layer, 1, hd]
        dec_s[hd, 0] = (jnp.where(diff >= 0, jnp.exp(lgf * jnp.maximum(diff, 0.0)), 0.0)
                        + jnp.where(diff <= 0, jnp.exp(lgb * jnp.maximum(-diff, 0.0)), 0.0))
        dec_s[hd, 1] = jnp.exp(lgf * (pos + 1.0))
        dec_s[hd, 2] = jnp.exp(lgb * (C - pos))
        dec_s[hd, 3] = jnp.exp(lgf * (C - 1.0 - post))
        dec_s[hd, 4] = jnp.exp(lgb * post)
        cdf.append(jnp.exp(lgf * cfull))
        cdb.append(jnp.exp(lgb * cfull))

    per_trip = 4
    items =[(sub, hd) for sub in range(per_trip) for hd in range(RET_HEADS)]
    cols = [slice(hd * LANES, (hd + 1) * LANES) for hd in range(RET_HEADS)]

    def p1(trip, carry):
        cs = [trip * per_trip + sub for sub in range(per_trip)]
        rs = [pl.ds(pl.multiple_of(c * C, C), C) for c in cs]
        kts = {(s, h): kt_ref[cs[s], h * RET_DK:(h + 1) * RET_DK, :] for s, h in items}
        vcs = {(s, h): v_ref[rs[s], cols[h]] for s, h in items}
        scores = {(s, h): _dot(q_ref[rs[s], cols[h]], kts[s, h]) for s, h in items}
        for s, h in items:
            kf = kts[s, h].astype(F32)
            kd = jnp.concatenate([(kf * dec_s[h, 3]).astype(BF16), (kf * dec_s[h, 4]).astype(BF16)], axis=0)
            u_s[cs[s], h] = _dot(kd, vcs[s, h])
        for s, h in items:
            acc_s[rs[s], cols[h]] = _dot((scores[s, h] * dec_s[h, 0]).astype(BF16), vcs[s, h])
        return carry

    lax.fori_loop(0, nseq * nc // per_trip, p1, 0)

    for sq in range(nseq):
        for hd in range(RET_HEADS):
            if has_state:
                init = (s0_ref[sq, 0, hd], s0_ref[sq, 1, hd])
            else:
                init = (jnp.zeros((RET_DK, RET_DV), F32), jnp.zeros((RET_DK, RET_DV), F32))

            def p2(t, carry, sq=sq, hd=hd):
                sf, sb = carry
                cf = sq * nc + t
                cb = sq * nc + (nc - 1 - t)
                st_s[cf, hd, :, 0:RET_DV] = sf.astype(BF16)
                st_s[cb, hd, :, RET_DV:2 * RET_DV] = sb.astype(BF16)
                sf = sf * cdf[hd] + u_s[cf, hd, 0:RET_DK, :]
                sb = sb * cdb[hd] + u_s[cb, hd, RET_DK:2 * RET_DK, :]
                return sf, sb

            sf, sb = lax.fori_loop(0, nc, p2, init)
            if not has_state:
                stout_ref[sq, 0, hd] = sf
                stout_ref[sq, 1, hd] = sb

    def p3(trip, carry):
        cs = [trip * per_trip + sub for sub in range(per_trip)]
        rs = [pl.ds(pl.multiple_of(c * C, C), C) for c in cs]
        crs = {(s, h): _dot(q_ref[rs[s], cols[h]], st_s[cs[s], h]) for s, h in items}
        for s, h in items:
            cr = crs[s, h]
            r = acc_s[rs[s], cols[h]] + cr[:, 0:RET_DV] * dec_s[h, 1] + cr[:, RET_DV:2 * RET_DV] * dec_s[h, 2]
            gh = g_ref[rs[s], cols[h]].astype(F32)
            o_ref[rs[s], cols[h]] = (_rms(r, RET_DV) * _silu(gh)).astype(BF16)
        return carry

    lax.fori_loop(0, nseq * nc // per_trip, p3, 0)


def _retention(lg, l, qr, kt, vr, g, s0, seq, nseq):
    rows = qr.shape[0]
    nb = rows // seq
    has_state = s0 is not None
    nct = nseq * seq // RET_CHUNK
    row = lambda b: (b, 0)
    if has_state:
        st_spec = pl.BlockSpec((nseq, None, 2, RET_HEADS, RET_DK, RET_DV), lambda b: (b, l, 0, 0, 0, 0))
    else:
        st_spec = pl.BlockSpec((nseq, 2, RET_HEADS, RET_DK, RET_DV), lambda b: (b, 0, 0, 0, 0))
    tok = pl.BlockSpec((nseq * seq, RET_WIDTH), row)
    in_specs = [pl.BlockSpec(memory_space=pltpu.SMEM), tok,
                pl.BlockSpec((nct, RET_WIDTH, RET_CHUNK), lambda b: (b, 0, 0)), tok, tok]
    args = [lg, qr, kt, vr, g]
    out_specs = [pl.BlockSpec((nseq * seq, RET_WIDTH), row)]
    out_shape = [jax.ShapeDtypeStruct((rows, RET_WIDTH), BF16)]
    if has_state:
        in_specs.append(st_spec)
        args.append(s0)
    else:
        out_specs.append(st_spec)
        out_shape.append(jax.ShapeDtypeStruct((nb, 2, RET_HEADS, RET_DK, RET_DV), F32))
    return pl.pallas_call(
        functools.partial(_ret_body, seq, nseq, has_state, l),
        grid=(nb // nseq,),
        in_specs=in_specs,
        out_specs=out_specs,
        out_shape=out_shape,
        scratch_shapes=[
            pltpu.VMEM((nseq * seq, RET_WIDTH), F32),
            pltpu.VMEM((nct, RET_HEADS, 2 * RET_DK, RET_DV), F32),
            pltpu.VMEM((nct, RET_HEADS, RET_DK, 2 * RET_DV), BF16),
            pltpu.VMEM((RET_HEADS, 5, RET_CHUNK, RET_CHUNK), F32),
        ],
        compiler_params=_params(("arbitrary",)),
        name="retention_lat" if has_state else "retention_ctx",
    )(*args)


def _mixer_residual(y_ref, a_ref, r_ref, wo_ref, g1_ref):
    out = _dot(a_ref[...], wo_ref[:MLA_WIDTH, :]) + _dot(r_ref[...], wo_ref[MLA_WIDTH:, :])
    return y_ref[...] + g1_ref[...] * out


def _ffn_in(y, gain_ref, sh_ref, sc_ref):
    h = _rms(y, D_MODEL) * gain_ref[...]
    return h * (1.0 + sc_ref[...]) + sh_ref[...]


def _ffn_body(y_ref, a_ref, r_ref, wo_ref, g1_ref, gain_ref, sh_ref, sc_ref, g2_ref, wg_ref, wu_ref, wd_ref, o_ref):
    tm = y_ref.shape[0]
    halves = [pl.ds(0, tm // 2), pl.ds(tm // 2, tm // 2)]
    mix = [_dot(a_ref[h, :], wo_ref[:MLA_WIDTH, :]) + _dot(r_ref[h, :], wo_ref[MLA_WIDTH:, :]) for h in halves]
    ys = [y_ref[h, :] + g1_ref[...] * m for h, m in zip(halves, mix)]
    hbs = [_ffn_in(y, gain_ref, sh_ref, sc_ref).astype(BF16) for y in ys]
    gus = [(_dot(hb, wg_ref[...]), _dot(hb, wu_ref[...])) for hb in hbs]
    acts = [(_silu(g) * u).astype(BF16) for g, u in gus]
    for h, y, act in zip(halves, ys, acts):
        o_ref[h, :] = y + g2_ref[...] * _dot(act, wd_ref[...])


def _ffn(y, a, r, st, wo, gain, wg, wu, wd, i, tm):
    rows = y.shape[0]
    l = st.layer
    row = lambda b: (b, 0)
    once = pl.Buffered(1)
    return pl.pallas_call(
        _ffn_body,
        grid=(rows // tm,),
        in_specs=[
            pl.BlockSpec((tm, D_MODEL), row),
            pl.BlockSpec((tm, MLA_WIDTH), row),
            pl.BlockSpec((tm, RET_WIDTH), row),
            _layer(wo, l, pipeline_mode=once), st.mod(2),
            _layer(gain, l), st.mod(3), st.mod(4), st.mod(5),
            _layer(wg, i, pipeline_mode=once), _layer(wu, i, pipeline_mode=once), _layer(wd, i, pipeline_mode=once),
        ],
        out_specs=pl.BlockSpec((tm, D_MODEL), row),
        out_shape=jax.ShapeDtypeStruct((rows, D_MODEL), F32),
        compiler_params=_params(("arbitrary",)),
        name="ffn",
    )(y, a, r, wo, st.mods, gain, st.mods, st.mods, st.mods, wg, wu, wd)


def _split_bf16(x):
    hi = x.astype(BF16)
    return hi, (x - hi.astype(F32)).astype(BF16)


def _route_body(y_ref, a_ref, r_ref, wo_ref, g1_ref, gain_ref, sh_ref, sc_ref, wr_ref,
                y1_ref, slot_ref, w_ref, cnt_ref, carry):
    tm = y_ref.shape[0]

    @pl.when(pl.program_id(0) == 0)
    def _():
        carry[...] = jnp.zeros_like(carry)

    half = tm // 2
    rows = [pl.ds(0, half), pl.ds(half, half)]
    mixes = [_dot(a_ref[rs, :], wo_ref[:MLA_WIDTH, :]) + _dot(r_ref[rs, :], wo_ref[MLA_WIDTH:, :]) for rs in rows]
    all_logits = []
    for rs, mix in zip(rows, mixes):
        y = y_ref[rs, :] + g1_ref[...] * mix
        y1_ref[rs, :] = y
        hi, lo = _split_bf16(_ffn_in(y, gain_ref, sh_ref, sc_ref))
        hh_hl = _dot(hi, wr_ref[...])
        all_logits.append(hh_hl[:, :LANES] + (_dot(lo, wr_ref[:, :LANES]) + hh_hl[:, LANES:]))

    lane = lax.broadcasted_iota(jnp.int32, (half, LANES), 1)
    ri = lax.broadcasted_iota(jnp.int32, (half, half), 0)
    ci = lax.broadcasted_iota(jnp.int32, (half, half), 1)
    lower = jnp.where(ri > ci, 1.0, 0.0).astype(BF16)
    ninf = jnp.float32(-jnp.inf)
    count = carry[...]
    for rs, logits in zip(rows, all_logits):
        lg = jnp.where(lane < N_EXPERTS, logits, ninf)
        m1 = jnp.max(lg, axis=-1, keepdims=True)
        i1 = jnp.min(jnp.where(lg == m1, lane, LANES), axis=-1, keepdims=True)
        lg2 = jnp.where(lane == i1, ninf, lg)
        m2 = jnp.max(lg2, axis=-1, keepdims=True)
        i2 = jnp.min(jnp.where(lg2 == m2, lane, LANES), axis=-1, keepdims=True)
        ex = jnp.exp(m2 - m1)
        den = 1.0 / (1.0 + ex)
        w_ref[rs, :] = jnp.where(lane == 0, den, 0.0) + jnp.where(lane == 1, ex * den, 0.0)

        sel1 = lane == i1
        sel2 = lane == i2
        onehot = jnp.where(sel1 | sel2, 1.0, 0.0)
        before = _dot(lower, onehot.astype(BF16)) + count
        r1 = jnp.sum(jnp.where(sel1, before, 0.0), axis=-1, keepdims=True)
        r2 = jnp.sum(jnp.where(sel2, before, 0.0), axis=-1, keepdims=True)
        slot_ref[rs, :] = (jnp.where(lane == 0, i1, 0) + jnp.where(lane == 1, i2, 0)
                           + jnp.where(lane == 2, r1.astype(jnp.int32), 0)
                           + jnp.where(lane == 3, r2.astype(jnp.int32), 0))
        count = count + jnp.sum(onehot, axis=0, keepdims=True)
    carry[...] = count
    cnt_ref[...] = count.astype(jnp.int32)


def _route(y, a, r, st, wo, gain, wr, i, tm):
    rows = y.shape[0]
    l = st.layer
    row = lambda b: (b, 0)
    return pl.pallas_call(
        _route_body,
        grid=(rows // tm,),
        in_specs=[
            pl.BlockSpec((tm, D_MODEL), row),
            pl.BlockSpec((tm, MLA_WIDTH), row),
            pl.BlockSpec((tm, RET_WIDTH), row),
            _layer(wo, l), st.mod(2),
            _layer(gain, l), st.mod(3), st.mod(4),
            _layer(wr, i),
        ],
        out_specs=[pl.BlockSpec((tm, D_MODEL), row), pl.BlockSpec((tm, LANES), row),
                   pl.BlockSpec((tm, LANES), row), pl.BlockSpec((1, LANES), lambda b: (0, 0))],
        out_shape=[jax.ShapeDtypeStruct((rows, D_MODEL), F32), jax.ShapeDtypeStruct((rows, LANES), jnp.int32),
                   jax.ShapeDtypeStruct((rows, LANES), F32), jax.ShapeDtypeStruct((1, LANES), jnp.int32)],
        scratch_shapes=[pltpu.VMEM((1, LANES), F32)],
        compiler_params=_params(("arbitrary",)),
        name="route",
    )(y, a, r, wo, st.mods, gain, st.mods, st.mods, wr)


def _dispatch_body(tr, dest_ref, zt_ref, y_ref, gain_ref, sh_ref, sc_ref, xs_hbm, stage, zeros, sem, zsem):
    i = pl.program_id(0)
    tm = y_ref.shape[0]

    @pl.when(i == 0)
    def _():
        zeros[...] = jnp.zeros_like(zeros)

        def ztile(z):
            r0 = pl.multiple_of(zt_ref[z] * (tr * SUB), tr * SUB)
            return pltpu.make_async_copy(zeros, xs_hbm.at[pl.ds(r0, tr * SUB)], zsem)

        for z in range(zt_ref.shape[0]):
            @pl.when(zt_ref[z] >= 0)
            def _():
                ztile(z).start()
        for z in range(zt_ref.shape[0]):
            @pl.when(zt_ref[z] >= 0)
            def _():
                ztile(z).wait()

    h = _ffn_in(y_ref[...], gain_ref, sh_ref, sc_ref)
    for s in range(D_MODEL // LANES):
        stage[pl.ds(s, tm, stride=SUB), :] = h[:, s * LANES:(s + 1) * LANES]

    def body(t, c):
        src = stage.at[pl.ds(pl.multiple_of(t * SUB, SUB), SUB)]
        for k in range(2):
            dst = dest_ref[(i * tm + t) * 2 + k]
            pltpu.make_async_copy(src, xs_hbm.at[pl.ds(pl.multiple_of(dst * SUB, SUB), SUB)],
                                  sem).start(priority=k)
        return c

    lax.fori_loop(0, tm, body, 0, unroll=8)

    for k in range(2):
        pltpu.make_async_copy(stage, stage, sem).wait()


def _dispatch(dest_flat, ztiles, y, st, gain, tm, ntile, tr):
    rows = y.shape[0]
    grid_spec = pltpu.PrefetchScalarGridSpec(
        num_scalar_prefetch=2,
        grid=(rows // tm,),
        in_specs=[
            pl.BlockSpec((tm, D_MODEL), lambda i, dr, cr: (i, 0)),
            _layer(gain, st.layer), st.mod(3), st.mod(4),
        ],
        out_specs=pl.BlockSpec(memory_space=pl.ANY),
        scratch_shapes=[
            pltpu.VMEM((tm * SUB, LANES), F32),
            pltpu.VMEM((tr * SUB, LANES), F32),
            pltpu.SemaphoreType.DMA(()),
            pltpu.SemaphoreType.DMA(()),
        ],
    )
    return pl.pallas_call(
        functools.partial(_dispatch_body, tr),
        grid_spec=grid_spec,
        out_shape=jax.ShapeDtypeStruct((ntile * tr * SUB, LANES), F32),
        compiler_params=_params(("arbitrary",), disable_bounds_checks=True),
        name="dispatch",
    )(dest_flat, ztiles, y, gain, st.mods, st.mods)


def _experts_body(tr, te_ref, nu_ref, xs_ref, wg_ref, wu_ref, wd_ref, o_ref):
    nt = D_MODEL // LANES
    used = pl.program_id(0) < nu_ref[0]

    @pl.when(used)
    def _():
        x = jnp.concatenate([xs_ref[pl.ds(s, tr, stride=SUB), :] for s in range(nt)], axis=1).astype(BF16)
        a = (_silu(_dot(x, wg_ref[0])) * _dot(x, wu_ref[0])).astype(BF16)
        o = _dot(a, wd_ref[0])
        for s in range(nt):
            o_ref[pl.ds(s, tr, stride=SUB), :] = o[:, s * LANES:(s + 1) * LANES]

    @pl.when(jnp.logical_not(used))
    def _():
        o_ref[...] = jnp.zeros_like(o_ref)


def _experts(tile_e, nused, xs, wg, wu, wd, i, tr):
    ntile = tile_e.shape[0]
    blk = lambda j, te, nu: (j, 0)
    wsel = lambda j, te, nu: (i, te[j], 0, 0)
    grid_spec = pltpu.PrefetchScalarGridSpec(
        num_scalar_prefetch=2,
        grid=(ntile,),
        in_specs=[
            pl.BlockSpec((tr * SUB, LANES), blk),
            pl.BlockSpec((None, 1, D_MODEL, EXPERT_FF), wsel),
            pl.BlockSpec((None, 1, D_MODEL, EXPERT_FF), wsel),
            pl.BlockSpec((None, 1, EXPERT_FF, D_MODEL), wsel),
        ],
        out_specs=pl.BlockSpec((tr * SUB, LANES), blk),
    )
    return pl.pallas_call(
        functools.partial(_experts_body, tr),
        grid_spec=grid_spec,
        out_shape=jax.ShapeDtypeStruct(xs.shape, F32),
        compiler_params=_params(("arbitrary",)),
        name="experts",
    )(tile_e, nused, xs, wg, wu, wd)


def _tile_plan(cnt, slots, ntile, tr):
    ptiles = (cnt + tr - 1) // tr
    tile_end = jnp.cumsum(ptiles)
    nused = tile_end[-1]
    off = (tile_end - ptiles) * tr
    experts = jnp.arange(N_EXPERTS, dtype=jnp.int32)
    dest = slots[:, 2:4] + jnp.sum(jnp.where(slots[:, 0:2, None] == experts, off, 0), axis=-1)
    j = jnp.minimum(jnp.arange(ntile), nused - 1)
    te = jnp.sum((j[:, None] >= tile_end[None, :]).astype(jnp.int32), axis=1)
    last = jnp.where(ptiles > 0, tile_end - 1, -1)
    tail = nused + jnp.arange(ntile - slots.shape[0] * 2 // tr)
    ztiles = jnp.concatenate([last, jnp.where(tail < ntile, tail, -1)])
    i32 = lambda a: a.astype(jnp.int32)
    return i32(dest).reshape(-1), i32(te), i32(nused).reshape(1), i32(ztiles)


def _combine_body(dest_ref, y_ref, g2_ref, w_ref, y2_hbm, o_ref, stage, sem):
    i = pl.program_id(0)
    tm = y_ref.shape[0]

    def body(t, c):
        for k in range(2):
            src = dest_ref[(i * tm + t) * 2 + k]
            pltpu.make_async_copy(y2_hbm.at[pl.ds(pl.multiple_of(src * SUB, SUB), SUB)],
                                  stage.at[k, pl.ds(pl.multiple_of(t * SUB, SUB), SUB)],
                                  sem).start(priority=k)
        return c

    lax.fori_loop(0, tm, body, 0, unroll=8)
    w = w_ref[...]
    w1 = w[:, 0:1]
    w2 = w[:, 1:2]
    for k in range(2):
        pltpu.make_async_copy(stage.at[k], stage.at[k], sem).wait()
    parts = []
    for s in range(D_MODEL // LANES):
        parts.append(w1 * stage[0, pl.ds(s, tm, stride=SUB), :] + w2 * stage[1, pl.ds(s, tm, stride=SUB), :])
    o_ref[...] = y_ref[...] + g2_ref[...] * jnp.concatenate(parts, axis=1)


def _combine(dest_flat, y, st, w, y2, tm):
    rows = y.shape[0]
    grid_spec = pltpu.PrefetchScalarGridSpec(
        num_scalar_prefetch=1,
        grid=(rows // tm,),
        in_specs=[
            pl.BlockSpec((tm, D_MODEL), lambda i, dr: (i, 0)),
            st.mod(5),
            pl.BlockSpec((tm, LANES), lambda i, dr: (i, 0)),
            pl.BlockSpec(memory_space=pl.ANY),
        ],
        out_specs=pl.BlockSpec((tm, D_MODEL), lambda i, dr: (i, 0)),
        scratch_shapes=[
            pltpu.VMEM((2, tm * SUB, LANES), F32),
            pltpu.SemaphoreType.DMA(()),
        ],
    )
    return pl.pallas_call(
        _combine_body,
        grid_spec=grid_spec,
        out_shape=jax.ShapeDtypeStruct((rows, D_MODEL), F32),
        compiler_params=_params(("arbitrary",), disable_bounds_checks=True),
        name="combine",
    )(dest_flat, y, st.mods, w, y2)


def _moe(y, a, r, st, wo, gain, wr, wg, wu, wd, i, tm, tr):
    rows = y.shape[0]
    ntile = 2 * rows // tr + N_EXPERTS
    y1, slots, w, cnt = _route(y, a, r, st, wo, gain, wr, i, tm)
    dest_flat, te, nused, ztiles = _tile_plan(cnt[0, :N_EXPERTS], slots[:, :4], ntile, tr)
    st_dma = _Stream(st.mods, st.layer, st.base, st.bpe * tm // DMA_ROW_BLOCK)
    xs = _dispatch(dest_flat, ztiles, y1, st_dma, gain, DMA_ROW_BLOCK, ntile, tr)
    y2 = _experts(te, nused, xs, wg, wu, wd, i, tr)
    return _combine(dest_flat, y1, st_dma, w, y2, DMA_ROW_BLOCK)


def _rope_tables(n_tok, dim):
    rows = n_tok // GRID_W
    row = jnp.repeat(jnp.arange(rows), GRID_W)
    col = jnp.tile(jnp.arange(GRID_W), rows)
    half = dim // 2
    freqs = ROPE_BASE ** (-jnp.arange(0, half, 2, dtype=F32) / half)

    def ang(p):
        a = p.astype(F32)[:, None] * freqs[None, :]
        return jnp.concatenate([a, a], axis=-1)

    angles = jnp.concatenate([ang(row), ang(col)], axis=-1)
    cos, sin = jnp.cos(angles), jnp.sin(angles)
    first = (np.arange(dim) % half) < (half // 2)
    sa = jnp.where(first, -sin, 0.0)
    sb = jnp.where(first, 0.0, sin)
    pad = LANES - dim
    cos = jnp.pad(cos, ((0, 0), (0, pad)), constant_values=1.0)
    sa = jnp.pad(sa, ((0, 0), (0, pad)))
    sb = jnp.pad(sb, ((0, 0), (0, pad)))
    return cos, sa, sb


def _pack_w_in(w_in):
    zeros = jnp.zeros(w_in.shape[:-1] + (LANES - QK_HEAD_DIM,), w_in.dtype)
    parts = []
    for hd in range(MLA_HEADS):
        b = hd * QK_HEAD_DIM
        parts += [w_in[..., b + QK_NOPE_DIM:b + QK_HEAD_DIM], w_in[..., b:b + QK_NOPE_DIM], zeros]
    o = MLA_HEADS * QK_HEAD_DIM
    parts.append(w_in[..., o:o + KV_LORA_RANK])
    o += KV_LORA_RANK
    parts += [w_in[..., o:o + QK_ROPE_DIM],
              jnp.zeros(w_in.shape[:-1] + (LANES - QK_ROPE_DIM,), w_in.dtype)]
    o += QK_ROPE_DIM
    parts += [w_in[..., o:o + RET_WIDTH], w_in[..., o + 2 * RET_WIDTH:]]
    w_kret_t = jnp.swapaxes(w_in[..., o + RET_WIDTH:o + 2 * RET_WIDTH], -1, -2)
    return jnp.concatenate(parts, axis=-1).astype(BF16), w_kret_t.astype(BF16)


def _pack_w_ukv(w_ukv):
    lead = w_ukv.shape[:-1]
    z32 = jnp.zeros(lead + (QK_ROPE_DIM,), w_ukv.dtype)
    z64 = jnp.zeros(lead + (V_HEAD_DIM,), w_ukv.dtype)
    kp, vp = [], []
    for hd in range(MLA_HEADS):
        b = hd * (QK_NOPE_DIM + V_HEAD_DIM)
        kp += [z32, w_ukv[..., b:b + QK_NOPE_DIM], z32]
        v = w_ukv[..., b + QK_NOPE_DIM:b + QK_NOPE_DIM + V_HEAD_DIM]
        vp += [v, z64] if hd % 2 == 0 else [z64, v]
    return jnp.concatenate(kp, axis=-1).astype(BF16), jnp.concatenate(vp, axis=-1).astype(BF16)


def _pack_head_gain(g):
    z = jnp.zeros(g.shape[:-1] + (LANES - QK_HEAD_DIM,), g.dtype)
    return jnp.concatenate([g[..., QK_NOPE_DIM:], g[..., :QK_NOPE_DIM], z], axis=-1)


def kernel(x_prompt, x_sample, cache_ckv, cache_krope, state_ret, c, c_ctx, attn_norm, ffn_norm, w_ada, b_ada, w_in, kv_norm, w_ukv, q_norm, k_norm, decay_logit, w_out, w_ffn_gate, w_ffn_up, w_ffn_down, w_router, w_exp_gate, w_exp_up, w_exp_down):
    nb_c, seq_c, _ = x_prompt.shape
    nb_l, seq_l, _ = x_sample.shape
    tm = ROW_BLOCK

    wp, wkt = _pack_w_in(w_in)
    wk, wv = _pack_w_ukv(w_ukv)
    wo = w_out.astype(BF16)
    wfg, wfu, wfd = w_ffn_gate.astype(BF16), w_ffn_up.astype(BF16), w_ffn_down.astype(BF16)
    weg, weu, wed = w_exp_gate.astype(BF16), w_exp_up.astype(BF16), w_exp_down.astype(BF16)
    wr = jnp.pad(w_router, ((0, 0), (0, 0), (0, LANES - N_EXPERTS)))
    wrh = wr.astype(BF16)
    wr2 = jnp.concatenate([wrh, (wr - wrh.astype(F32)).astype(BF16)], axis=-1)
    qg = _pack_head_gain(q_norm)[:, None, :]
    kg = _pack_head_gain(k_norm)[:, None, :]
    kvg = kv_norm[:, None, :]
    lg = jax.nn.log_sigmoid(decay_logit.astype(F32))
    tabs_m = _rope_tables(seq_l, QK_ROPE_DIM)
    tabs_r = _rope_tables(seq_l, RET_DK)
    tabs_rt = tuple(t.T for t in tabs_r)
    cache_kr_pad = jnp.pad(cache_krope, ((0, 0), (0, 0), (0, 0), (0, LANES - QK_ROPE_DIM)))

    cs = jnp.concatenate([c_ctx[None, :], c, jnp.zeros((SUB - 1 - nb_l, D_MODEL), F32)], axis=0)
    mods = _adaln(cs, w_ada, b_ada).reshape(DEPTH, SUB, 6, 1, D_MODEL)

    yp = x_prompt.reshape(nb_c * seq_c, D_MODEL)
    ys = x_sample.reshape(nb_l * seq_l, D_MODEL)
    ckv_list, krope_list, state_list = [], [], []
    an, fn = attn_norm[:, None, :], ffn_norm[:, None, :]
    for l in range(DEPTH):
        for ctx in (True, False):
            if ctx:
                y, st = yp, _Stream(mods, l, 0, nb_c * seq_c // tm)
            else:
                y, st = ys, _Stream(mods, l, 1, seq_l // tm)
            q, ckv, kr, qr, kret, vr, g = _inproj(
                y, st, an, wp, wkt, qg, kvg, None if ctx else tabs_m + tabs_r + tabs_rt, tm)
            if ctx:
                a = _attn_ctx(q, ckv, kr, wk, wv, kg, l, seq_c)
                r, state = _retention(lg, l, qr, kret, vr, g, None, seq_c, RET_CTX_SEQS)
                ckv_list.append(ckv.reshape(nb_c, seq_c, KV_LORA_RANK))
                krope_list.append(kr[:, :QK_ROPE_DIM].reshape(nb_c, seq_c, QK_ROPE_DIM))
                state_list.append(state)
            else:
                a = _attn_lat(q, ckv, kr, cache_ckv, cache_kr_pad, wk, wv, kg, l, tabs_m, seq_l, Q_BLOCK)
                r, = _retention(lg, l, qr, kret, vr, g, state_ret, seq_l, 1)
            i = l // 2
            if l % 2 == 0:
                y = _ffn(y, a, r, st, wo, fn, wfg, wfu, wfd, i, tm)
            else:
                y = _moe(y, a, r, st, wo, fn, wr2, weg, weu, wed, i, tm, EXPERT_TILE)
            if ctx:
                yp = y
            else:
                ys = y
    return (yp.reshape(nb_c, seq_c, D_MODEL), ys.reshape(nb_l, seq_l, D_MODEL),
            jnp.stack(ckv_list, axis=1), jnp.stack(krope_list, axis=1), jnp.stack(state_list, axis=1))
```

```python
import functools
import math

import jax
import jax.numpy as jnp
import numpy as np
from jax import lax
from jax.experimental import pallas as pl
from jax.experimental.pallas import tpu as pltpu

D_MODEL = 1024
DEPTH = 4
GRID_W = 64
MLA_HEADS = 8
QK_NOPE_DIM = 64
QK_ROPE_DIM = 32
QK_HEAD_DIM = QK_NOPE_DIM + QK_ROPE_DIM
V_HEAD_DIM = 64
KV_LORA_RANK = 256
MLA_WIDTH = MLA_HEADS * V_HEAD_DIM
RET_HEADS = 4
RET_DK = 128
RET_DV = 128
RET_CHUNK = 128
RET_WIDTH = RET_HEADS * RET_DV
D_FF = 2816
N_EXPERTS = 8
EXPERT_FF = 1408
ROPE_BASE = 10000.0
EPS = 1e-6

LANES = 128
SUB = 8
QP_WIDTH = MLA_HEADS * LANES
OFF_Q = 0
OFF_CKV = OFF_Q + QP_WIDTH
OFF_KR = OFF_CKV + KV_LORA_RANK
OFF_QR = OFF_KR + LANES
OFF_VR = OFF_QR + RET_WIDTH
OFF_G = OFF_VR + RET_WIDTH
IN_COLS_P = OFF_G + RET_WIDTH

VMEM_LIMIT = 56 * 1024 * 1024
LOG2E = math.log2(math.e)

ROW_BLOCK = 512
DMA_ROW_BLOCK = 1024
Q_BLOCK = 512
EXPERT_TILE = 256
RET_CTX_SEQS = 4

F32 = jnp.float32
BF16 = jnp.bfloat16


def _dot(a, b):
    return jnp.dot(a, b, preferred_element_type=F32)


def _dot_nt(a, b):
    return lax.dot_general(a, b, (((1,), (1,)), ((), ())), preferred_element_type=F32)


def _dot_tn(a, b):
    return lax.dot_general(a, b, (((0,), (0,)), ((), ())), preferred_element_type=F32)


def _rms(x, n):
    return x * lax.rsqrt(jnp.sum(x * x, axis=-1, keepdims=True) * (1.0 / n) + EPS)


def _silu(x):
    return x * (1.0 / (1.0 + jnp.exp(-x)))


def _rope(t, cos, sa, sb, q):
    return t * cos + pltpu.roll(t, LANES - q, 1) * sa + pltpu.roll(t, q, 1) * sb


def _rope_t(t, cos, sa, sb, q):
    n = t.shape[0]
    up = jnp.concatenate([t[q:], t[:q]], axis=0)
    down = jnp.concatenate([t[n - q:], t[:n - q]], axis=0)
    return t * cos + up * sa + down * sb


def _params(sem, **kw):
    return pltpu.CompilerParams(dimension_semantics=sem, vmem_limit_bytes=VMEM_LIMIT, **kw)


def _layer(arr, idx, **kw):
    nd = arr.ndim
    return pl.BlockSpec((None,) + arr.shape[1:], lambda *g: (idx,) + (0,) * (nd - 1), **kw)


class _Stream:
    def __init__(self, mods, layer, base, blocks_per_entry):
        self.mods, self.layer, self.base, self.bpe = mods, layer, base, blocks_per_entry

    def mod(self, which):
        l, base, bpe = self.layer, self.base, self.bpe
        return pl.BlockSpec((None, None, None, 1, D_MODEL), lambda i, *_: (l, base + i // bpe, which, 0, 0))


def _adaln_body(c_ref, w_ref, b_ref, o_ref):
    x = _silu(c_ref[...]).astype(BF16)
    o_ref[0] = _dot(x, w_ref[0].astype(BF16)) + b_ref[0]


def _adaln(cs, w_ada, b_ada):
    tn = 1536
    nb = cs.shape[0]
    return pl.pallas_call(
        _adaln_body,
        grid=(DEPTH, 6 * D_MODEL // tn),
        in_specs=[
            pl.BlockSpec((nb, D_MODEL), lambda l, j: (0, 0)),
            pl.BlockSpec((1, D_MODEL, tn), lambda l, j: (l, 0, j)),
            pl.BlockSpec((1, 1, tn), lambda l, j: (l, 0, j)),
        ],
        out_specs=pl.BlockSpec((1, nb, tn), lambda l, j: (l, 0, j)),
        out_shape=jax.ShapeDtypeStruct((DEPTH, nb, 6 * D_MODEL), F32),
        compiler_params=_params(("arbitrary", "arbitrary")),
        name="adaln",
    )(cs, w_ada, b_ada.reshape(DEPTH, 1, 6 * D_MODEL))


def _inproj_body(rope, y_ref, gain_ref, sh_ref, sc_ref, w_ref, wkt_ref, qg_ref, kvg_ref, *rest):
    if rope:
        tab_ref, tabt_ref = rest[:2]
        rest = rest[2:]
        row_tab = lambda j: (lambda rs: tab_ref[rs, j * LANES:(j + 1) * LANES])
        col_tab = lambda j: (lambda rs: tabt_ref[j * RET_DK:(j + 1) * RET_DK, rs])
        cm, sam, sbm, cr, sar, sbr = (row_tab(j) for j in range(6))
        crt, sart, sbrt = (col_tab(j) for j in range(3))
    q_ref, ckv_ref, kr_ref, qr_ref, kt_ref, vr_ref, g_ref = rest
    tm = y_ref.shape[0]

    half = tm // 2
    rows = [pl.ds(0, half), pl.ds(half, half)]

    hbs = []
    for rs in rows:
        h = _rms(y_ref[rs, :], D_MODEL) * gain_ref[...]
        hbs.append((h * (1.0 + sc_ref[...]) + sh_ref[...]).astype(BF16))

    prods = []
    for rs, hb in zip(rows, hbs):
        pq = _dot(hb, w_ref[:, OFF_Q:OFF_Q + QP_WIDTH])
        pckv = _dot(hb, w_ref[:, OFF_CKV:OFF_CKV + KV_LORA_RANK])
        kr_ref[rs, :] = _dot(hb, w_ref[:, OFF_KR:OFF_KR + LANES])
        pqr = _dot(hb, w_ref[:, OFF_QR:OFF_QR + RET_WIDTH])
        kt = _dot_nt(wkt_ref[...], hb)
        vr_ref[rs, :] = _dot(hb, w_ref[:, OFF_VR:OFF_VR + RET_WIDTH]).astype(BF16)
        g_ref[rs, :] = _dot(hb, w_ref[:, OFF_G:OFF_G + RET_WIDTH]).astype(BF16)
        prods.append((pq, pckv, pqr, kt))

    qg = qg_ref[...]
    for part, (rs, (pq, pckv, pqr, kt)) in enumerate(zip(rows, prods)):
        for hd in range(MLA_HEADS):
            t = _rms(pq[:, hd * LANES:(hd + 1) * LANES], QK_HEAD_DIM) * qg
            if rope:
                t = _rope(t, cm(rs), sam(rs), sbm(rs), QK_ROPE_DIM // 4)
            q_ref[rs, hd * LANES:(hd + 1) * LANES] = t.astype(BF16)

        ckv_ref[rs, :] = _rms(pckv, KV_LORA_RANK) * kvg_ref[...]

        for hd in range(RET_HEADS):
            c0 = hd * LANES
            t = pqr[:, c0:c0 + LANES]
            if rope:
                t = _rope(t, cr(rs), sar(rs), sbr(rs), RET_DK // 4)
            qr_ref[rs, c0:c0 + LANES] = t.astype(BF16)
        for hd in range(RET_HEADS):
            t = kt[hd * RET_DK:(hd + 1) * RET_DK, :]
            if rope:
                t = _rope_t(t, crt(rs), sart(rs), sbrt(rs), RET_DK // 4)
            t = (t * (RET_DK ** -0.5)).astype(BF16)
            for j in range(half // RET_CHUNK):
                kt_ref[part * (half // RET_CHUNK) + j, hd * RET_DK:(hd + 1) * RET_DK, :] = (
                    t[:, j * RET_CHUNK:(j + 1) * RET_CHUNK])


def _inproj(y, st, gain, wp, wkt, qg, kvg, tables, tm):
    rows = y.shape[0]
    rope = tables is not None
    l, bpb = st.layer, st.bpe
    cpb = tm // RET_CHUNK
    in_specs = [
        pl.BlockSpec((tm, D_MODEL), lambda i: (i, 0)),
        _layer(gain, l), st.mod(0), st.mod(1), _layer(wp, l), _layer(wkt, l), _layer(qg, l), _layer(kvg, l),
    ]
    args = [y, gain, st.mods, st.mods, wp, wkt, qg, kvg]
    if rope:
        tab, tabt = tables
        in_specs += [pl.BlockSpec((tm, tab.shape[1]), lambda i: (i % bpb, 0)),
                     pl.BlockSpec((tabt.shape[0], tm), lambda i: (0, i % bpb))]
        args += [tab, tabt]
    blocks = [(tm, QP_WIDTH), (tm, KV_LORA_RANK), (tm, LANES), (tm, RET_WIDTH),
              (cpb, RET_WIDTH, RET_CHUNK), (tm, RET_WIDTH), (tm, RET_WIDTH)]
    dtypes = (BF16, F32, F32, BF16, BF16, BF16, BF16)
    out_specs, out_shape = [], []
    for blk, dt in zip(blocks, dtypes):
        nd = len(blk)
        out_specs.append(pl.BlockSpec(blk, lambda i, nd=nd: (i,) + (0,) * (nd - 1)))
        out_shape.append(jax.ShapeDtypeStruct((rows // tm * blk[0],) + blk[1:], dt))
    return pl.pallas_call(
        functools.partial(_inproj_body, rope),
        grid=(rows // tm,),
        in_specs=in_specs,
        out_specs=out_specs,
        out_shape=out_shape,
        compiler_params=_params(("arbitrary",)),
        name="inproj_rope" if rope else "inproj",
    )(*args)


def _kv_up(ckv_f32, wk_ref, wv_ref):
    cb = ckv_f32.astype(BF16)
    return _dot(cb, wk_ref[...]), _dot(cb, wv_ref[...])


def _make_kv(up, kr_pad, kg, rope_tabs):
    kall, vall = up
    pe = kr_pad * kg
    if rope_tabs is not None:
        pe = _rope(pe, *rope_tabs, QK_ROPE_DIM // 4)
    ss_pe = jnp.sum(kr_pad * kr_pad, axis=-1, keepdims=True)
    ks, vs = [], []
    for hd in range(MLA_HEADS):
        kn = kall[:, hd * LANES:(hd + 1) * LANES]
        ss = jnp.sum(kn * kn, axis=-1, keepdims=True) + ss_pe
        k = (kn * kg + pe) * lax.rsqrt(ss * (1.0 / QK_HEAD_DIM) + EPS)
        ks.append(k.astype(BF16))
        vs.append(vall[:, hd * LANES:(hd + 1) * LANES].astype(BF16))
    return ks, vs


def _softmax_pv(s, v):
    m = jnp.max(s, axis=-1, keepdims=True)
    p = jnp.exp2((s - m) * (QK_HEAD_DIM ** -0.5 * LOG2E))
    l = jnp.sum(p, axis=-1, keepdims=True)
    return _dot(p.astype(BF16), v) * (1.0 / l)


def _attend(q_ref, get_k, get_v, o_ref):
    def scores(hd):
        return _dot_nt(q_ref[:, hd * LANES:(hd + 1) * LANES], get_k(hd))

    s_next = scores(0)
    acc = None
    for hd in range(MLA_HEADS):
        s = s_next
        if hd + 1 < MLA_HEADS:
            s_next = scores(hd + 1)
        o = _softmax_pv(s, get_v(hd))
        acc = o if hd % 2 == 0 else acc + o
        if hd % 2 == 1:
            o_ref[:, (hd // 2) * LANES:(hd // 2 + 1) * LANES] = acc.astype(BF16)


def _attn_ctx_body(q_ref, ckv_ref, kr_ref, wk_ref, wv_ref, kg_ref, o_ref):
    ks, vs = _make_kv(_kv_up(ckv_ref[...], wk_ref, wv_ref), kr_ref[...], kg_ref[...], None)
    _attend(q_ref, lambda hd: ks[hd], lambda hd: vs[hd], o_ref)


def _attn_ctx(q, ckv, kr, wk, wv, kg, l, seq):
    rows = q.shape[0]
    row = lambda b: (b, 0)
    return pl.pallas_call(
        _attn_ctx_body,
        grid=(rows // seq,),
        in_specs=[
            pl.BlockSpec((seq, QP_WIDTH), row),
            pl.BlockSpec((seq, KV_LORA_RANK), row),
            pl.BlockSpec((seq, LANES), row),
            _layer(wk, l), _layer(wv, l), _layer(kg, l),
        ],
        out_specs=pl.BlockSpec((seq, MLA_WIDTH), row),
        out_shape=jax.ShapeDtypeStruct((rows, MLA_WIDTH), BF16),
        compiler_params=_params(("arbitrary",)),
        name="attn_ctx",
    )(q, ckv, kr, wk, wv, kg)


def _attn_lat_body(past, seq, tq, q_ref, ckv_ref, kr_ref, cckv_ref, ckr_ref, wk_ref, wv_ref, kg_ref,
                   cm_ref, sam_ref, sbm_ref, o_ref, k_s, v_s):
    kc = 256

    @pl.when(pl.program_id(1) == 0)
    def _():
        kg = kg_ref[...]
        for c in range(past // kc):
            r = slice(c * kc, (c + 1) * kc)
            ks, vs = _make_kv(_kv_up(cckv_ref[r, :], wk_ref, wv_ref), ckr_ref[r, :], kg, None)
            for hd in range(MLA_HEADS):
                k_s[hd, r, :] = ks[hd]
                v_s[hd, r, :] = vs[hd]

        def chunk(c, carry):
            r0 = pl.multiple_of(c * kc, kc)
            rs = pl.ds(r0, kc)
            tabs = (cm_ref[rs, :], sam_ref[rs, :], sbm_ref[rs, :])
            ks, vs = _make_kv(_kv_up(ckv_ref[rs, :], wk_ref, wv_ref), kr_ref[rs, :], kg, tabs)
            ro = pl.ds(past + r0, kc)
            for hd in range(MLA_HEADS):
                k_s[hd, ro, :] = ks[hd]
                v_s[hd, ro, :] = vs[hd]
            return carry

        lax.fori_loop(0, seq // kc, chunk, 0)

    _attend(q_ref, lambda hd: k_s[hd], lambda hd: v_s[hd], o_ref)


def _attn_lat(q, ckv, kr, cache_ckv, cache_kr, wk, wv, kg, l, tabs_m, seq, tq):
    rows = q.shape[0]
    nb = rows // seq
    nq = seq // tq
    past = cache_ckv.shape[2]
    const = lambda b, i: (0, 0)
    return pl.pallas_call(
        functools.partial(_attn_lat_body, past, seq, tq),
        grid=(nb, nq),
        in_specs=[
            pl.BlockSpec((tq, QP_WIDTH), lambda b, i: (b * nq + i, 0)),
            pl.BlockSpec((seq, KV_LORA_RANK), lambda b, i: (b, 0)),
            pl.BlockSpec((seq, LANES), lambda b, i: (b, 0)),
            pl.BlockSpec((None, None, past, KV_LORA_RANK), lambda b, i: (b, l, 0, 0)),
            pl.BlockSpec((None, None, past, LANES), lambda b, i: (b, l, 0, 0)),
            _layer(wk, l), _layer(wv, l), _layer(kg, l),
            pl.BlockSpec((seq, LANES), const),
            pl.BlockSpec((seq, LANES), const),
            pl.BlockSpec((seq, LANES), const),
        ],
        out_specs=pl.BlockSpec((tq, MLA_WIDTH), lambda b, i: (b * nq + i, 0)),
        out_shape=jax.ShapeDtypeStruct((rows, MLA_WIDTH), BF16),
        scratch_shapes=[
            pltpu.VMEM((MLA_HEADS, past + seq, LANES), BF16),
            pltpu.VMEM((MLA_HEADS, past + seq, LANES), BF16),
        ],
        compiler_params=_params(("arbitrary", "arbitrary")),
        name="attn_lat",
    )(q, ckv, kr, cache_ckv, cache_kr, wk, wv, kg, *tabs_m)


def _ret_body(seq, nseq, has_state, layer, lg_ref, q_ref, kt_ref, v_ref, g_ref, *rest):
    if has_state:
        s0_ref, o_ref, acc_s, u_s, st_s, dec_s = rest
    else:
        o_ref, stout_ref, acc_s, u_s, st_s, dec_s = rest
    C = RET_CHUNK
    nc = seq // C
    ii = lax.broadcasted_iota(jnp.int32, (C, C), 0)
    jj = lax.broadcasted_iota(jnp.int32, (C, C), 1)
    diff = (ii - jj).astype(F32)
    pos = ii.astype(F32)
    post = jj.astype(F32)
    cfull = jnp.full((1, LANES), float(C), F32)
    cdf, cdb = [], []
    for hd in range(RET_HEADS):
        lgf = lg_ref[layer, 0, hd]
        lgb = lg_ref[layer, 1, hd]
        dec_s[hd, 0] = (jnp.where(diff >= 0, jnp.exp(lgf * jnp.maximum(diff, 0.0)), 0.0)
                        + jnp.where(diff <= 0, jnp.exp(lgb * jnp.maximum(-diff, 0.0)), 0.0))
        dec_s[hd, 1] = jnp.exp(lgf * (pos + 1.0))
        dec_s[hd, 2] = jnp.exp(lgb * (C - pos))
        dec_s[hd, 3] = jnp.exp(lgf * (C - 1.0 - post))
        dec_s[hd, 4] = jnp.exp(lgb * post)
        cdf.append(jnp.exp(lgf * cfull))
        cdb.append(jnp.exp(lgb * cfull))

    per_trip = 4
    items =[(sub, hd) for sub in range(per_trip) for hd in range(RET_HEADS)]
    cols = [slice(hd * LANES, (hd + 1) * LANES) for hd in range(RET_HEADS)]

    def p1(trip, carry):
        cs = [trip * per_trip + sub for sub in range(per_trip)]
        rs = [pl.ds(pl.multiple_of(c * C, C), C) for c in cs]
        kts = {(s, h): kt_ref[cs[s], h * RET_DK:(h + 1) * RET_DK, :] for s, h in items}
        vcs = {(s, h): v_ref[rs[s], cols[h]] for s, h in items}
        scores = {(s, h): _dot(q_ref[rs[s], cols[h]], kts[s, h]) for s, h in items}
        for s, h in items:
            kf = kts[s, h].astype(F32)
            kd = jnp.concatenate([(kf * dec_s[h, 3]).astype(BF16), (kf * dec_s[h, 4]).astype(BF16)], axis=0)
            u_s[cs[s], h] = _dot(kd, vcs[s, h])
        for s, h in items:
            acc_s[rs[s], cols[h]] = _dot((scores[s, h] * dec_s[h, 0]).astype(BF16), vcs[s, h])
        return carry

    lax.fori_loop(0, nseq * nc // per_trip, p1, 0)

    for sq in range(nseq):
        for hd in range(RET_HEADS):
            if has_state:
                init = (s0_ref[sq, 0, hd], s0_ref[sq, 1, hd])
            else:
                init = (jnp.zeros((RET_DK, RET_DV), F32), jnp.zeros((RET_DK, RET_DV), F32))

            def p2(t, carry, sq=sq, hd=hd):
                sf, sb = carry
                cf = sq * nc + t
                cb = sq * nc + (nc - 1 - t)
                st_s[cf, hd, :, 0:RET_DV] = sf.astype(BF16)
                st_s[cb, hd, :, RET_DV:2 * RET_DV] = sb.astype(BF16)
                sf = sf * cdf[hd] + u_s[cf, hd, 0:RET_DK, :]
                sb = sb * cdb[hd] + u_s[cb, hd, RET_DK:2 * RET_DK, :]
                return sf, sb

            sf, sb = lax.fori_loop(0, nc, p2, init)
            if not has_state:
                stout_ref[sq, 0, hd] = sf
                stout_ref[sq, 1, hd] = sb

    def p3(trip, carry):
        cs = [trip * per_trip + sub for sub in range(per_trip)]
        rs = [pl.ds(pl.multiple_of(c * C, C), C) for c in cs]
        crs = {(s, h): _dot(q_ref[rs[s], cols[h]], st_s[cs[s], h]) for s, h in items}
        for s, h in items:
            cr = crs[s, h]
            r = acc_s[rs[s], cols[h]] + cr[:, 0:RET_DV] * dec_s[h, 1] + cr[:, RET_DV:2 * RET_DV] * dec_s[h, 2]
            gh = g_ref[rs[s], cols[h]].astype(F32)
            o_ref[rs[s], cols[h]] = (_rms(r, RET_DV) * _silu(gh)).astype(BF16)
        return carry

    lax.fori_loop(0, nseq * nc // per_trip, p3, 0)


def _retention(lg, l, qr, kt, vr, g, s0, seq, nseq):
    rows = qr.shape[0]
    nb = rows // seq
    has_state = s0 is not None
    nct = nseq * seq // RET_CHUNK
    row = lambda b: (b, 0)
    if has_state:
        st_spec = pl.BlockSpec((nseq, None, 2, RET_HEADS, RET_DK, RET_DV), lambda b: (b, l, 0, 0, 0, 0))
    else:
        st_spec = pl.BlockSpec((nseq, 2, RET_HEADS, RET_DK, RET_DV), lambda b: (b, 0, 0, 0, 0))
    tok = pl.BlockSpec((nseq * seq, RET_WIDTH), row)
    in_specs = [pl.BlockSpec(memory_space=pltpu.SMEM), tok,
                pl.BlockSpec((nct, RET_WIDTH, RET_CHUNK), lambda b: (b, 0, 0)), tok, tok]
    args = [lg, qr, kt, vr, g]
    out_specs = [pl.BlockSpec((nseq * seq, RET_WIDTH), row)]
    out_shape = [jax.ShapeDtypeStruct((rows, RET_WIDTH), BF16)]
    if has_state:
        in_specs.append(st_spec)
        args.append(s0)
    else:
        out_specs.append(st_spec)
        out_shape.append(jax.ShapeDtypeStruct((nb, 2, RET_HEADS, RET_DK, RET_DV), F32))
    return pl.pallas_call(
        functools.partial(_ret_body, seq, nseq, has_state, l),
        grid=(nb // nseq,),
        in_specs=in_specs,
        out_specs=out_specs,
        out_shape=out_shape,
        scratch_shapes=[
            pltpu.VMEM((nseq * seq, RET_WIDTH), F32),
            pltpu.VMEM((nct, RET_HEADS, 2 * RET_DK, RET_DV), F32),
            pltpu.VMEM((nct, RET_HEADS, RET_DK, 2 * RET_DV), BF16),
            pltpu.VMEM((RET_HEADS, 5, RET_CHUNK, RET_CHUNK), F32),
        ],
        compiler_params=_params(("arbitrary",)),
        name="retention_lat" if has_state else "retention_ctx",
    )(*args)


def _mixer_residual(y_ref, a_ref, r_ref, wo_ref, g1_ref):
    out = _dot(a_ref[...], wo_ref[:MLA_WIDTH, :]) + _dot(r_ref[...], wo_ref[MLA_WIDTH:, :])
    return y_ref[...] + g1_ref[...] * out


def _ffn_in(y, gain_ref, sh_ref, sc_ref):
    h = _rms(y, D_MODEL) * gain_ref[...]
    return h * (1.0 + sc_ref[...]) + sh_ref[...]


def _ffn_body(y_ref, a_ref, r_ref, wo_ref, g1_ref, gain_ref, sh_ref, sc_ref, g2_ref, wg_ref, wu_ref, wd_ref, o_ref):
    tm = y_ref.shape[0]
    halves = [pl.ds(0, tm // 2), pl.ds(tm // 2, tm // 2)]
    mix = [_dot(a_ref[h, :], wo_ref[:MLA_WIDTH, :]) + _dot(r_ref[h, :], wo_ref[MLA_WIDTH:, :]) for h in halves]
    ys = [y_ref[h, :] + g1_ref[...] * m for h, m in zip(halves, mix)]
    hbs = [_ffn_in(y, gain_ref, sh_ref, sc_ref).astype(BF16) for y in ys]
    gus = [(_dot(hb, wg_ref[...]), _dot(hb, wu_ref[...])) for hb in hbs]
    acts = [(_silu(g) * u).astype(BF16) for g, u in gus]
    for h, y, act in zip(halves, ys, acts):
        o_ref[h, :] = y + g2_ref[...] * _dot(act, wd_ref[...])


def _ffn(y, a, r, st, wo, gain, wg, wu, wd, i, tm):
    rows = y.shape[0]
    l = st.layer
    row = lambda b: (b, 0)
    once = pl.Buffered(1)
    return pl.pallas_call(
        _ffn_body,
        grid=(rows // tm,),
        in_specs=[
            pl.BlockSpec((tm, D_MODEL), row),
            pl.BlockSpec((tm, MLA_WIDTH), row),
            pl.BlockSpec((tm, RET_WIDTH), row),
            _layer(wo, l, pipeline_mode=once), st.mod(2),
            _layer(gain, l), st.mod(3), st.mod(4), st.mod(5),
            _layer(wg, i, pipeline_mode=once), _layer(wu, i, pipeline_mode=once), _layer(wd, i, pipeline_mode=once),
        ],
        out_specs=pl.BlockSpec((tm, D_MODEL), row),
        out_shape=jax.ShapeDtypeStruct((rows, D_MODEL), F32),
        compiler_params=_params(("arbitrary",)),
        name="ffn",
    )(y, a, r, wo, st.mods, gain, st.mods, st.mods, st.mods, wg, wu, wd)


def _split_bf16(x):
    hi = x.astype(BF16)
    return hi, (x - hi.astype(F32)).astype(BF16)


def _route_body(y_ref, a_ref, r_ref, wo_ref, g1_ref, gain_ref, sh_ref, sc_ref, wr_ref,
                y1_ref, slot_ref, w_ref, cnt_ref, carry):
    tm = y_ref.shape[0]

    @pl.when(pl.program_id(0) == 0)
    def _():
        carry[...] = jnp.zeros_like(carry)

    half = tm // 2
    rows = [pl.ds(0, half), pl.ds(half, half)]
    mixes = [_dot(a_ref[rs, :], wo_ref[:MLA_WIDTH, :]) + _dot(r_ref[rs, :], wo_ref[MLA_WIDTH:, :]) for rs in rows]
    all_logits = []
    for rs, mix in zip(rows, mixes):
        y = y_ref[rs, :] + g1_ref[...] * mix
        y1_ref[rs, :] = y
        hi, lo = _split_bf16(_ffn_in(y, gain_ref, sh_ref, sc_ref))
        hh_hl = _dot(hi, wr_ref[...])
        all_logits.append(hh_hl[:, :LANES] + (_dot(lo, wr_ref[:, :LANES]) + hh_hl[:, LANES:]))

    lane = lax.broadcasted_iota(jnp.int32, (half, LANES), 1)
    ri = lax.broadcasted_iota(jnp.int32, (half, half), 0)
    ci = lax.broadcasted_iota(jnp.int32, (half, half), 1)
    lower = jnp.where(ri > ci, 1.0, 0.0).astype(BF16)
    ninf = jnp.float32(-jnp.inf)
    count = carry[...]
    for rs, logits in zip(rows, all_logits):
        lg = jnp.where(lane < N_EXPERTS, logits, ninf)
        m1 = jnp.max(lg, axis=-1, keepdims=True)
        i1 = jnp.min(jnp.where(lg == m1, lane, LANES), axis=-1, keepdims=True)
        lg2 = jnp.where(lane == i1, ninf, lg)
        m2 = jnp.max(lg2, axis=-1, keepdims=True)
        i2 = jnp.min(jnp.where(lg2 == m2, lane, LANES), axis=-1, keepdims=True)
        ex = jnp.exp(m2 - m1)
        den = 1.0 / (1.0 + ex)
        w_ref[rs, :] = jnp.where(lane == 0, den, 0.0) + jnp.where(lane == 1, ex * den, 0.0)

        sel1 = lane == i1
        sel2 = lane == i2
        onehot = jnp.where(sel1 | sel2, 1.0, 0.0)
        before = _dot(lower, onehot.astype(BF16)) + count
        r1 = jnp.sum(jnp.where(sel1, before, 0.0), axis=-1, keepdims=True)
        r2 = jnp.sum(jnp.where(sel2, before, 0.0), axis=-1, keepdims=True)
        slot_ref[rs, :] = (jnp.where(lane == 0, i1, 0) + jnp.where(lane == 1, i2, 0)
                           + jnp.where(lane == 2, r1.astype(jnp.int32), 0)
                           + jnp.where(lane == 3, r2.astype(jnp.int32), 0))
        count = count + jnp.sum(onehot, axis=0, keepdims=True)
    carry[...] = count
    cnt_ref[...] = count.astype(jnp.int32)


def _route(y, a, r, st, wo, gain, wr, i, tm):
    rows = y.shape[0]
    l = st.layer
    row = lambda b: (b, 0)
    return pl.pallas_call(
        _route_body,
        grid=(rows // tm,),
        in_specs=[
            pl.BlockSpec((tm, D_MODEL), row),
            pl.BlockSpec((tm, MLA_WIDTH), row),
            pl.BlockSpec((tm, RET_WIDTH), row),
            _layer(wo, l), st.mod(2),
            _layer(gain, l), st.mod(3), st.mod(4),
            _layer(wr, i),
        ],
        out_specs=[pl.BlockSpec((tm, D_MODEL), row), pl.BlockSpec((tm, LANES), row),
                   pl.BlockSpec((tm, LANES), row), pl.BlockSpec((1, LANES), lambda b: (0, 0))],
        out_shape=[jax.ShapeDtypeStruct((rows, D_MODEL), F32), jax.ShapeDtypeStruct((rows, LANES), jnp.int32),
                   jax.ShapeDtypeStruct((rows, LANES), F32), jax.ShapeDtypeStruct((1, LANES), jnp.int32)],
        scratch_shapes=[pltpu.VMEM((1, LANES), F32)],
        compiler_params=_params(("arbitrary",)),
        name="route",
    )(y, a, r, wo, st.mods, gain, st.mods, st.mods, wr)


def _dispatch_body(tr, dest_ref, zt_ref, y_ref, gain_ref, sh_ref, sc_ref, xs_hbm, stage, zeros, sem, zsem):
    i = pl.program_id(0)
    tm = y_ref.shape[0]

    @pl.when(i == 0)
    def _():
        zeros[...] = jnp.zeros_like(zeros)

        def ztile(z):
            r0 = pl.multiple_of(zt_ref[z] * (tr * SUB), tr * SUB)
            return pltpu.make_async_copy(zeros, xs_hbm.at[pl.ds(r0, tr * SUB)], zsem)

        for z in range(zt_ref.shape[0]):
            @pl.when(zt_ref[z] >= 0)
            def _():
                ztile(z).start()
        for z in range(zt_ref.shape[0]):
            @pl.when(zt_ref[z] >= 0)
            def _():
                ztile(z).wait()

    h = _ffn_in(y_ref[...], gain_ref, sh_ref, sc_ref)
    for s in range(D_MODEL // LANES):
        stage[pl.ds(s, tm, stride=SUB), :] = h[:, s * LANES:(s + 1) * LANES]

    def body(t, c):
        src = stage.at[pl.ds(pl.multiple_of(t * SUB, SUB), SUB)]
        for k in range(2):
            dst = dest_ref[(i * tm + t) * 2 + k]
            pltpu.make_async_copy(src, xs_hbm.at[pl.ds(pl.multiple_of(dst * SUB, SUB), SUB)],
                                  sem).start(priority=k)
        return c

    lax.fori_loop(0, tm, body, 0, unroll=8)

    for k in range(2):
        pltpu.make_async_copy(stage, stage, sem).wait()


def _dispatch(dest_flat, ztiles, y, st, gain, tm, ntile, tr):
    rows = y.shape[0]
    grid_spec = pltpu.PrefetchScalarGridSpec(
        num_scalar_prefetch=2,
        grid=(rows // tm,),
        in_specs=[
            pl.BlockSpec((tm, D_MODEL), lambda i, dr, cr: (i, 0)),
            _layer(gain, st.layer), st.mod(3), st.mod(4),
        ],
        out_specs=pl.BlockSpec(memory_space=pl.ANY),
        scratch_shapes=[
            pltpu.VMEM((tm * SUB, LANES), F32),
            pltpu.VMEM((tr * SUB, LANES), F32),
            pltpu.SemaphoreType.DMA(()),
            pltpu.SemaphoreType.DMA(()),
        ],
    )
    return pl.pallas_call(
        functools.partial(_dispatch_body, tr),
        grid_spec=grid_spec,
        out_shape=jax.ShapeDtypeStruct((ntile * tr * SUB, LANES), F32),
        compiler_params=_params(("arbitrary",), disable_bounds_checks=True),
        name="dispatch",
    )(dest_flat, ztiles, y, gain, st.mods, st.mods)


def _experts_body(tr, te_ref, nu_ref, xs_ref, wg_ref, wu_ref, wd_ref, o_ref):
    nt = D_MODEL // LANES
    used = pl.program_id(0) < nu_ref[0]

    @pl.when(used)
    def _():
        x = jnp.concatenate([xs_ref[pl.ds(s, tr, stride=SUB), :] for s in range(nt)], axis=1).astype(BF16)
        a = (_silu(_dot(x, wg_ref[0])) * _dot(x, wu_ref[0])).astype(BF16)
        o = _dot(a, wd_ref[0])
        for s in range(nt):
            o_ref[pl.ds(s, tr, stride=SUB), :] = o[:, s * LANES:(s + 1) * LANES]

    @pl.when(jnp.logical_not(used))
    def _():
        o_ref[...] = jnp.zeros_like(o_ref)


def _experts(tile_e, nused, xs, wg, wu, wd, i, tr):
    ntile = tile_e.shape[0]
    blk = lambda j, te, nu: (j, 0)
    wsel = lambda j, te, nu: (i, te[j], 0, 0)
    grid_spec = pltpu.PrefetchScalarGridSpec(
        num_scalar_prefetch=2,
        grid=(ntile,),
        in_specs=[
            pl.BlockSpec((tr * SUB, LANES), blk),
            pl.BlockSpec((None, 1, D_MODEL, EXPERT_FF), wsel),
            pl.BlockSpec((None, 1, D_MODEL, EXPERT_FF), wsel),
            pl.BlockSpec((None, 1, EXPERT_FF, D_MODEL), wsel),
        ],
        out_specs=pl.BlockSpec((tr * SUB, LANES), blk),
    )
    return pl.pallas_call(
        functools.partial(_experts_body, tr),
        grid_spec=grid_spec,
        out_shape=jax.ShapeDtypeStruct(xs.shape, F32),
        compiler_params=_params(("arbitrary",)),
        name="experts",
    )(tile_e, nused, xs, wg, wu, wd)


def _tile_plan(cnt, slots, ntile, tr):
    ptiles = (cnt + tr - 1) // tr
    tile_end = jnp.cumsum(ptiles)
    nused = tile_end[-1]
    off = (tile_end - ptiles) * tr
    experts = jnp.arange(N_EXPERTS, dtype=jnp.int32)
    dest = slots[:, 2:4] + jnp.sum(jnp.where(slots[:, 0:2, None] == experts, off, 0), axis=-1)
    j = jnp.minimum(jnp.arange(ntile), nused - 1)
    te = jnp.sum((j[:, None] >= tile_end[None, :]).astype(jnp.int32), axis=1)
    last = jnp.where(ptiles > 0, tile_end - 1, -1)
    tail = nused + jnp.arange(ntile - slots.shape[0] * 2 // tr)
    ztiles = jnp.concatenate([last, jnp.where(tail < ntile, tail, -1)])
    i32 = lambda a: a.astype(jnp.int32)
    return i32(dest).reshape(-1), i32(te), i32(nused).reshape(1), i32(ztiles)


def _combine_body(dest_ref, y_ref, g2_ref, w_ref, y2_hbm, o_ref, stage, sem):
    i = pl.program_id(0)
    tm = y_ref.shape[0]

    def body(t, c):
        for k in range(2):
            src = dest_ref[(i * tm + t) * 2 + k]
            pltpu.make_async_copy(y2_hbm.at[pl.ds(pl.multiple_of(src * SUB, SUB), SUB)],
                                  stage.at[k, pl.ds(pl.multiple_of(t * SUB, SUB), SUB)],
                                  sem).start(priority=k)
        return c

    lax.fori_loop(0, tm, body, 0, unroll=8)
    w = w_ref[...]
    w1 = w[:, 0:1]
    w2 = w[:, 1:2]
    for k in range(2):
        pltpu.make_async_copy(stage.at[k], stage.at[k], sem).wait()
    parts = []
    for s in range(D_MODEL // LANES):
        parts.append(w1 * stage[0, pl.ds(s, tm, stride=SUB), :] + w2 * stage[1, pl.ds(s, tm, stride=SUB), :])
    o_ref[...] = y_ref[...] + g2_ref[...] * jnp.concatenate(parts, axis=1)


def _combine(dest_flat, y, st, w, y2, tm):
    rows = y.shape[0]
    grid_spec = pltpu.PrefetchScalarGridSpec(
        num_scalar_prefetch=1,
        grid=(rows // tm,),
        in_specs=[
            pl.BlockSpec((tm, D_MODEL), lambda i, dr: (i, 0)),
            st.mod(5),
            pl.BlockSpec((tm, LANES), lambda i, dr: (i, 0)),
            pl.BlockSpec(memory_space=pl.ANY),
        ],
        out_specs=pl.BlockSpec((tm, D_MODEL), lambda i, dr: (i, 0)),
        scratch_shapes=[
            pltpu.VMEM((2, tm * SUB, LANES), F32),
            pltpu.SemaphoreType.DMA(()),
        ],
    )
    return pl.pallas_call(
        _combine_body,
        grid_spec=grid_spec,
        out_shape=jax.ShapeDtypeStruct((rows, D_MODEL), F32),
        compiler_params=_params(("arbitrary",), disable_bounds_checks=True),
        name="combine",
    )(dest_flat, y, st.mods, w, y2)


def _moe(y, a, r, st, wo, gain, wr, wg, wu, wd, i, tm, tr):
    rows = y.shape[0]
    ntile = 2 * rows // tr + N_EXPERTS
    y1, slots, w, cnt = _route(y, a, r, st, wo, gain, wr, i, tm)
    dest_flat, te, nused, ztiles = _tile_plan(cnt[0, :N_EXPERTS], slots[:, :4], ntile, tr)
    st_dma = _Stream(st.mods, st.layer, st.base, st.bpe * tm // DMA_ROW_BLOCK)
    xs = _dispatch(dest_flat, ztiles, y1, st_dma, gain, DMA_ROW_BLOCK, ntile, tr)
    y2 = _experts(te, nused, xs, wg, wu, wd, i, tr)
    return _combine(dest_flat, y1, st_dma, w, y2, DMA_ROW_BLOCK)


def _rope_tables(n_tok, dim):
    rows = n_tok // GRID_W
    row = jnp.repeat(jnp.arange(rows), GRID_W)
    col = jnp.tile(jnp.arange(GRID_W), rows)
    half = dim // 2
    freqs = ROPE_BASE ** (-jnp.arange(0, half, 2, dtype=F32) / half)

    def ang(p):
        a = p.astype(F32)[:, None] * freqs[None, :]
        return jnp.concatenate([a, a], axis=-1)

    angles = jnp.concatenate([ang(row), ang(col)], axis=-1)
    cos, sin = jnp.cos(angles), jnp.sin(angles)
    first = (np.arange(dim) % half) < (half // 2)
    sa = jnp.where(first, -sin, 0.0)
    sb = jnp.where(first, 0.0, sin)
    pad = LANES - dim
    cos = jnp.pad(cos, ((0, 0), (0, pad)), constant_values=1.0)
    sa = jnp.pad(sa, ((0, 0), (0, pad)))
    sb = jnp.pad(sb, ((0, 0), (0, pad)))
    return cos, sa, sb


def _pack_w_in(w_in):
    zeros = jnp.zeros(w_in.shape[:-1] + (LANES - QK_HEAD_DIM,), w_in.dtype)
    parts = []
    for hd in range(MLA_HEADS):
        b = hd * QK_HEAD_DIM
        parts += [w_in[..., b + QK_NOPE_DIM:b + QK_HEAD_DIM], w_in[..., b:b + QK_NOPE_DIM], zeros]
    o = MLA_HEADS * QK_HEAD_DIM
    parts.append(w_in[..., o:o + KV_LORA_RANK])
    o += KV_LORA_RANK
    parts += [w_in[..., o:o + QK_ROPE_DIM],
              jnp.zeros(w_in.shape[:-1] + (LANES - QK_ROPE_DIM,), w_in.dtype)]
    o += QK_ROPE_DIM
    parts += [w_in[..., o:o + RET_WIDTH], w_in[..., o + 2 * RET_WIDTH:]]
    w_kret_t = jnp.swapaxes(w_in[..., o + RET_WIDTH:o + 2 * RET_WIDTH], -1, -2)
    return jnp.concatenate(parts, axis=-1).astype(BF16), w_kret_t.astype(BF16)


def _pack_w_ukv(w_ukv):
    lead = w_ukv.shape[:-1]
    z32 = jnp.zeros(lead + (QK_ROPE_DIM,), w_ukv.dtype)
    z64 = jnp.zeros(lead + (V_HEAD_DIM,), w_ukv.dtype)
    kp, vp = [], []
    for hd in range(MLA_HEADS):
        b = hd * (QK_NOPE_DIM + V_HEAD_DIM)
        kp += [z32, w_ukv[..., b:b + QK_NOPE_DIM], z32]
        v = w_ukv[..., b + QK_NOPE_DIM:b + QK_NOPE_DIM + V_HEAD_DIM]
        vp += [v, z64] if hd % 2 == 0 else [z64, v]
    return jnp.concatenate(kp, axis=-1).astype(BF16), jnp.concatenate(vp, axis=-1).astype(BF16)


def _pack_head_gain(g):
    z = jnp.zeros(g.shape[:-1] + (LANES - QK_HEAD_DIM,), g.dtype)
    return jnp.concatenate([g[..., QK_NOPE_DIM:], g[..., :QK_NOPE_DIM], z], axis=-1)


def kernel(x_prompt, x_sample, cache_ckv, cache_krope, state_ret, c, c_ctx, attn_norm, ffn_norm, w_ada, b_ada, w_in, kv_norm, w_ukv, q_norm, k_norm, decay_logit, w_out, w_ffn_gate, w_ffn_up, w_ffn_down, w_router, w_exp_gate, w_exp_up, w_exp_down):
    nb_c, seq_c, _ = x_prompt.shape
    nb_l, seq_l, _ = x_sample.shape
    tm = ROW_BLOCK

    wp, wkt = _pack_w_in(w_in)
    wk, wv = _pack_w_ukv(w_ukv)
    wo = w_out.astype(BF16)
    wfg, wfu, wfd = w_ffn_gate.astype(BF16), w_ffn_up.astype(BF16), w_ffn_down.astype(BF16)
    weg, weu, wed = w_exp_gate.astype(BF16), w_exp_up.astype(BF16), w_exp_down.astype(BF16)
    wr = jnp.pad(w_router, ((0, 0), (0, 0), (0, LANES - N_EXPERTS)))
    wrh = wr.astype(BF16)
    wr2 = jnp.concatenate([wrh, (wr - wrh.astype(F32)).astype(BF16)], axis=-1)
    qg = _pack_head_gain(q_norm)[:, None, :]
    kg = _pack_head_gain(k_norm)[:, None, :]
    kvg = kv_norm[:, None, :]
    lg = jax.nn.log_sigmoid(decay_logit.astype(F32))
    tabs_m = _rope_tables(seq_l, QK_ROPE_DIM)
    tabs_r = _rope_tables(seq_l, RET_DK)
    rope_tabs = (jnp.concatenate(tabs_m + tabs_r, axis=1), jnp.concatenate([t.T for t in tabs_r], axis=0))
    cache_kr_pad = jnp.pad(cache_krope, ((0, 0), (0, 0), (0, 0), (0, LANES - QK_ROPE_DIM)))

    cs = jnp.concatenate([c_ctx[None, :], c, jnp.zeros((SUB - 1 - nb_l, D_MODEL), F32)], axis=0)
    mods = _adaln(cs, w_ada, b_ada).reshape(DEPTH, SUB, 6, 1, D_MODEL)

    yp = x_prompt.reshape(nb_c * seq_c, D_MODEL)
    ys = x_sample.reshape(nb_l * seq_l, D_MODEL)
    ckv_list, krope_list, state_list = [], [], []
    an, fn = attn_norm[:, None, :], ffn_norm[:, None, :]
    for l in range(DEPTH):
        for ctx in (True, False):
            if ctx:
                y, st = yp, _Stream(mods, l, 0, nb_c * seq_c // tm)
            else:
                y, st = ys, _Stream(mods, l, 1, seq_l // tm)
            q, ckv, kr, qr, kret, vr, g = _inproj(
                y, st, an, wp, wkt, qg, kvg, None if ctx else rope_tabs, tm)
            if ctx:
                a = _attn_ctx(q, ckv, kr, wk, wv, kg, l, seq_c)
                r, state = _retention(lg, l, qr, kret, vr, g, None, seq_c, RET_CTX_SEQS)
                ckv_list.append(ckv.reshape(nb_c, seq_c, KV_LORA_RANK))
                krope_list.append(kr[:, :QK_ROPE_DIM].reshape(nb_c, seq_c, QK_ROPE_DIM))
                state_list.append(state)
            else:
                a = _attn_lat(q, ckv, kr, cache_ckv, cache_kr_pad, wk, wv, kg, l, tabs_m, seq_l, Q_BLOCK)
                r, = _retention(lg, l, qr, kret, vr, g, state_ret, seq_l, 1)
            i = l // 2
            if l % 2 == 0:
                y = _ffn(y, a, r, st, wo, fn, wfg, wfu, wfd, i, tm)
            else:
                y = _moe(y, a, r, st, wo, fn, wr2, weg, weu, wed, i, tm, EXPERT_TILE)
            if ctx:
                yp = y
            else:
                ys = y
    return (yp.reshape(nb_c, seq_c, D_MODEL), ys.reshape(nb_l, seq_l, D_MODEL),
            jnp.stack(ckv_list, axis=1), jnp.stack(krope_list, axis=1), jnp.stack(state_list, axis=1))
```
